```python
import math
import jax, jax.numpy as jnp
from jax import lax
import numpy as np

D_MODEL = 1024
BATCH = 4
SEQ = 4096
DEPTH = 2
DEC_BATCH = 128
DEC_SEQ = 4
PAST_LEN = 2048
PAGE_SIZE = 128

N_A_LAYERS = DEPTH // 2
N_B_LAYERS = DEPTH - N_A_LAYERS
GLA_HEADS = 4
GLA_DK = D_MODEL // 2 // GLA_HEADS
GLA_DV = D_MODEL // GLA_HEADS
GLA_RANK = 16
GLA_TAU = 16.0
GLA_CHUNK = 64
FOX_HEADS = 16
FOX_HD = D_MODEL // FOX_HEADS
Q_BLOCK = 128
D_FF = 2816
N_SUB = 3
NORM_EPS = 1e-6

kernel_name = 'yoco_gla_fox_macaron_adaln_step'


def rmsnorm(x, g):
    xf = x.astype(jnp.float32)
    y = xf * lax.rsqrt(jnp.mean(xf * xf, axis=-1, keepdims=True) + NORM_EPS)
    return (y * g.astype(jnp.float32)).astype(x.dtype)


def modulate(x, g, shift, scale):
    return rmsnorm(x, g) * (1 + scale[:, None, :]) + shift[:, None, :]


def swiglu(u, w_up, w_down):
    a, b = jnp.split(u @ w_up, 2, axis=-1)
    return (jax.nn.silu(a) * b) @ w_down


def gla_chunk_scan(q, k, v, g, s0):
    B, L = q.shape[:2]
    C = math.gcd(L, GLA_CHUNK)
    n = L // C

    def to_chunks(t):
        return t.reshape(B, n, C, t.shape[2], t.shape[3]).transpose(1, 0, 3, 2, 4)

    qc, kc, vc, gc = to_chunks(q), to_chunks(k), to_chunks(v), to_chunks(g)
    causal = jnp.tril(jnp.ones((C, C), bool))[:, :, None]

    def step(S, inp):
        qi, ki, vi, gi = inp
        b = jnp.cumsum(gi, axis=2)
        b_last = b[:, :, -1:, :]
        o_inter = jnp.einsum('bhcd,bhde->bhce', qi * jnp.exp(b), S)
        diff = b[:, :, :, None, :] - b[:, :, None, :, :]
        decay = jnp.exp(jnp.where(causal, diff, -jnp.inf))
        att = jnp.einsum('bhid,bhjd,bhijd->bhij', qi, ki, decay)
        o = o_inter + jnp.einsum('bhij,bhje->bhie', att, vi)
        S = jnp.exp(b_last)[:, :, 0, :, None] * S + jnp.einsum('bhcd,bhce->bhde', ki * jnp.exp(b_last - b), vi)
        return S, o

    S, o = lax.scan(step, s0, (qc, kc, vc, gc))
    o = o.transpose(1, 0, 3, 2, 4).reshape(B, L, GLA_HEADS, GLA_DV)
    return o, S


def gla_mixer(u, s0, w_in, w_gate2, b_gate, g_out, w_out):
    B, L, _ = u.shape
    dk = GLA_HEADS * GLA_DK
    dv = GLA_HEADS * GLA_DV
    proj = u @ w_in
    q, k, v, r, glr = jnp.split(proj, [dk, 2 * dk, 2 * dk + dv, 2 * dk + 2 * dv], axis=-1)
    log_a = jax.nn.log_sigmoid((glr @ w_gate2 + b_gate).astype(jnp.float32)) / GLA_TAU
    f32 = jnp.float32
    o, S = gla_chunk_scan(
        q.astype(f32).reshape(B, L, GLA_HEADS, GLA_DK) * GLA_DK ** -0.5,
        k.astype(f32).reshape(B, L, GLA_HEADS, GLA_DK),
        v.astype(f32).reshape(B, L, GLA_HEADS, GLA_DV),
        log_a.reshape(B, L, GLA_HEADS, GLA_DK),
        s0.astype(f32))
    o = rmsnorm(o, g_out).reshape(B, L, dv).astype(u.dtype) * jax.nn.silu(r)
    return o @ w_out, S.astype(s0.dtype)


def shared_kv(h, shift, scale, g_kv, w_kvf, b_f, g_k):
    B, L, _ = h.shape
    dfox = FOX_HEADS * FOX_HD
    kvf = modulate(h, g_kv, shift, scale) @ w_kvf
    k, v, fl = jnp.split(kvf, [dfox, 2 * dfox], axis=-1)
    k = rmsnorm(k.reshape(B, L, FOX_HEADS, FOX_HD), g_k)
    v = v.reshape(B, L, FOX_HEADS, FOX_HD)
    logf = jax.nn.log_sigmoid((fl + b_f).astype(jnp.float32))
    return k, v, logf


def fox_prompt_attend(q, k, v, logf):
    B, L = q.shape[:2]
    Ft = jnp.cumsum(logf.astype(jnp.float32), axis=1).transpose(0, 2, 1)
    nb = L // Q_BLOCK
    qb = q.reshape(B, nb, Q_BLOCK, FOX_HEADS, FOX_HD).transpose(1, 0, 2, 3, 4)
    Fb = Ft.reshape(B, FOX_HEADS, nb, Q_BLOCK).transpose(2, 0, 1, 3)
    kpos = jnp.arange(L)
    scale = FOX_HD ** -0.5

    def block(args):
        i, qi, Fi = args
        qpos = i * Q_BLOCK + jnp.arange(Q_BLOCK)
        s = jnp.einsum('bqhd,bkhd->bhqk', qi, k).astype(jnp.float32) * scale
        s = s + (Fi[:, :, :, None] - Ft[:, :, None, :])
        s = jnp.where(kpos[None, :] <= qpos[:, None], s, -jnp.inf)
        p = jax.nn.softmax(s, axis=-1)
        return jnp.einsum('bhqk,bkhd->bqhd', p.astype(v.dtype), v)

    o = lax.map(block, (jnp.arange(nb), qb, Fb))
    return o.transpose(1, 0, 2, 3, 4).reshape(B, L, FOX_HEADS, FOX_HD)


def fox_paged_attend(q, k_new, v_new, logf_new, cache_k, cache_v, cache_logf, page_table):
    DB, T = q.shape[:2]
    kp = cache_k[page_table].reshape(DB, -1, FOX_HEADS, FOX_HD)
    vp = cache_v[page_table].reshape(DB, -1, FOX_HEADS, FOX_HD)
    lp = cache_logf[page_table].reshape(DB, -1, FOX_HEADS)
    P = kp.shape[1]
    F_past = jnp.cumsum(lp.astype(jnp.float32), axis=1)
    F_new = F_past[:, -1:, :] + jnp.cumsum(logf_new.astype(jnp.float32), axis=1)
    Fq = F_new.transpose(0, 2, 1)[:, :, :, None]
    scale = FOX_HD ** -0.5
    s_past = jnp.einsum('bqhd,bkhd->bhqk', q, kp).astype(jnp.float32) * scale \
        + Fq - F_past.transpose(0, 2, 1)[:, :, None, :]
    s_new = jnp.einsum('bqhd,bkhd->bhqk', q, k_new).astype(jnp.float32) * scale \
        + Fq - F_new.transpose(0, 2, 1)[:, :, None, :]
    s_new = jnp.where(jnp.tril(jnp.ones((T, T), bool)), s_new, -jnp.inf)
    p = jax.nn.softmax(jnp.concatenate([s_past, s_new], axis=-1), axis=-1).astype(vp.dtype)
    o = jnp.einsum('bhqk,bkhd->bqhd', p[..., :P], vp) \
        + jnp.einsum('bhqk,bkhd->bqhd', p[..., P:].astype(v_new.dtype), v_new)
    return o.astype(q.dtype)


def fox_mixer(u, kv, attend, w_qg, g_q, w_o):
    B, L, _ = u.shape
    k, v, logf = kv
    q, og = jnp.split(u @ w_qg, 2, axis=-1)
    q = rmsnorm(q.reshape(B, L, FOX_HEADS, FOX_HD), g_q)
    o = attend(q, k, v, logf).reshape(B, L, FOX_HEADS * FOX_HD) * jax.nn.sigmoid(og)
    return o @ w_o


def trunk(x, c, gla_init, attend, P):
    Bn = x.shape[0]
    sc = jax.nn.silu(c)
    h = x
    gla_states = []
    kv = None
    for layer in range(DEPTH):
        mod = (sc @ P['w_ada'][layer] + P['b_ada'][layer]).reshape(Bn, N_SUB, 3, D_MODEL)
        u = modulate(h, P['g_norm'][layer, 0], mod[:, 0, 0], mod[:, 0, 1])
        h = h + 0.5 * mod[:, 0, 2][:, None, :] * swiglu(u, P['w_ffn_up'][layer, 0], P['w_ffn_down'][layer, 0])
        u = modulate(h, P['g_norm'][layer, 1], mod[:, 1, 0], mod[:, 1, 1])
        if layer < N_A_LAYERS:
            a = layer
            mix, S = gla_mixer(u, gla_init[a], P['gla_w_in'][a], P['gla_w_gate2'][a],
                               P['gla_b_gate'][a], P['gla_g_out'][a], P['gla_w_out'][a])
            gla_states.append(S)
        else:
            b = layer - N_A_LAYERS
            mix = fox_mixer(u, kv, attend, P['fox_w_qg'][b], P['fox_g_q'][b], P['fox_w_o'][b])
        h = h + mod[:, 1, 2][:, None, :] * mix
        u = modulate(h, P['g_norm'][layer, 2], mod[:, 2, 0], mod[:, 2, 1])
        h = h + 0.5 * mod[:, 2, 2][:, None, :] * swiglu(u, P['w_ffn_up'][layer, 1], P['w_ffn_down'][layer, 1])
        if layer == N_A_LAYERS - 1:
            shift_kv, scale_kv = jnp.split(sc @ P['w_ada_kv'] + P['b_ada_kv'], 2, axis=-1)
            kv = shared_kv(h, shift_kv, scale_kv, P['g_kv'], P['w_kvf'], P['b_f'], P['g_k'])
    k, v, logf = kv
    return h, jnp.stack(gla_states), k, v, logf.astype(x.dtype)


def setup_inputs(seed: int = 0) -> dict:
    key = jax.random.key(seed)
    keys = iter(jax.random.split(key, 40))

    def nrm(shape, s):
        return jax.random.normal(next(keys), shape, jnp.float32) * s

    n_pages = PAST_LEN // PAGE_SIZE
    used = DEC_BATCH * n_pages
    n_phys = used + max(1, used // 4)
    dk = GLA_HEADS * GLA_DK
    dv = GLA_HEADS * GLA_DV
    dfox = FOX_HEADS * FOX_HD
    page_table = jax.random.permutation(next(keys), n_phys)[:used].reshape(DEC_BATCH, n_pages).astype(jnp.int32)
    return {
        'x_prompt': nrm((BATCH, SEQ, D_MODEL), 1.0),
        'x_sample': nrm((DEC_BATCH, DEC_SEQ, D_MODEL), 1.0),
        'state_gla': nrm((N_A_LAYERS, DEC_BATCH, GLA_HEADS, GLA_DK, GLA_DV), 0.5),
        'cache_k': nrm((n_phys, PAGE_SIZE, FOX_HEADS, FOX_HD), 1.0),
        'cache_v': nrm((n_phys, PAGE_SIZE, FOX_HEADS, FOX_HD), 1.0),
        'cache_logf': jax.nn.log_sigmoid(nrm((n_phys, PAGE_SIZE, FOX_HEADS), 1.0)),
        'page_table': page_table,
        'c_prompt': nrm((BATCH, D_MODEL), 1.0),
        'c_sample': nrm((DEC_BATCH, D_MODEL), 1.0),
        'w_ada': nrm((DEPTH, D_MODEL, N_SUB * 3 * D_MODEL), D_MODEL ** -0.5),
        'b_ada': nrm((DEPTH, N_SUB * 3 * D_MODEL), 0.02),
        'g_norm': 1.0 + nrm((DEPTH, N_SUB, D_MODEL), 0.02),
        'w_ffn_up': nrm((DEPTH, 2, D_MODEL, 2 * D_FF), D_MODEL ** -0.5),
        'w_ffn_down': nrm((DEPTH, 2, D_FF, D_MODEL), D_FF ** -0.5),
        'gla_w_in': nrm((N_A_LAYERS, D_MODEL, 2 * dk + 2 * dv + GLA_RANK), D_MODEL ** -0.5),
        'gla_w_gate2': nrm((N_A_LAYERS, GLA_RANK, dk), GLA_RANK ** -0.5),
        'gla_b_gate': nrm((N_A_LAYERS, dk), 0.1),
        'gla_g_out': 1.0 + nrm((N_A_LAYERS, GLA_DV), 0.02),
        'gla_w_out': nrm((N_A_LAYERS, dv, D_MODEL), dv ** -0.5),
        'w_ada_kv': nrm((D_MODEL, 2 * D_MODEL), D_MODEL ** -0.5),
        'b_ada_kv': nrm((2 * D_MODEL,), 0.02),
        'g_kv': 1.0 + nrm((D_MODEL,), 0.02),
        'w_kvf': nrm((D_MODEL, 2 * dfox + FOX_HEADS), D_MODEL ** -0.5),
        'b_f': nrm((FOX_HEADS,), 0.1),
        'g_k': 1.0 + nrm((FOX_HD,), 0.02),
        'fox_w_qg': nrm((N_B_LAYERS, D_MODEL, 2 * dfox), D_MODEL ** -0.5),
        'fox_g_q': 1.0 + nrm((N_B_LAYERS, FOX_HD), 0.02),
        'fox_w_o': nrm((N_B_LAYERS, dfox, D_MODEL), dfox ** -0.5),
    }


def reference(x_prompt, x_sample, state_gla, cache_k, cache_v, cache_logf, page_table, c_prompt, c_sample,
              w_ada, b_ada, g_norm, w_ffn_up, w_ffn_down,
              gla_w_in, gla_w_gate2, gla_b_gate, gla_g_out, gla_w_out,
              w_ada_kv, b_ada_kv, g_kv, w_kvf, b_f, g_k,
              fox_w_qg, fox_g_q, fox_w_o):
    P = dict(w_ada=w_ada, b_ada=b_ada, g_norm=g_norm, w_ffn_up=w_ffn_up, w_ffn_down=w_ffn_down,
             gla_w_in=gla_w_in, gla_w_gate2=gla_w_gate2, gla_b_gate=gla_b_gate,
             gla_g_out=gla_g_out, gla_w_out=gla_w_out,
             w_ada_kv=w_ada_kv, b_ada_kv=b_ada_kv, g_kv=g_kv, w_kvf=w_kvf, b_f=b_f, g_k=g_k,
             fox_w_qg=fox_w_qg, fox_g_q=fox_g_q, fox_w_o=fox_w_o)
    gla_zero = jnp.zeros((N_A_LAYERS, x_prompt.shape[0], GLA_HEADS, GLA_DK, GLA_DV), x_prompt.dtype)
    y_prompt, sg_prompt, k_prompt, v_prompt, lf_prompt = trunk(x_prompt, c_prompt, gla_zero, fox_prompt_attend, P)

    def paged_attend(q, k, v, lf):
        return fox_paged_attend(q, k, v, lf, cache_k, cache_v, cache_logf, page_table)

    y_sample, sg_sample, k_sample, v_sample, lf_sample = trunk(x_sample, c_sample, state_gla, paged_attend, P)
    return (y_prompt, y_sample, sg_prompt, k_prompt, v_prompt, lf_prompt, sg_sample, k_sample, v_sample, lf_sample)
```

```python
import functools

import jax
import jax.numpy as jnp
from jax import lax
from jax.experimental import pallas as pl
from jax.experimental.pallas import tpu as pltpu

F32 = jnp.float32
BF16 = jnp.bfloat16

D_MODEL = 1024
BATCH = 4
SEQ = 4096
DEC_BATCH = 128
DEC_SEQ = 4
PAST_LEN = 2048
PAGE_SIZE = 128
N_PAGES = PAST_LEN // PAGE_SIZE
GLA_HEADS = 4
GLA_DK = 128
GLA_DV = 256
GLA_RANK = 16
GLA_TAU = 16.0
FOX_HEADS = 16
FOX_HD = 64
D_FF = 2816
NORM_EPS = 1e-6
HD_SHIFT = FOX_HD.bit_length() - 1
HEAD_SHIFT = FOX_HEADS.bit_length() - 1

DK_ALL = GLA_HEADS * GLA_DK
DV_ALL = GLA_HEADS * GLA_DV
QKVR = 2 * DK_ALL + 2 * DV_ALL
LANE = 128
SUBLANE = 8

NEG = -1e30
VMEM_LIMIT = 56 * 1024 * 1024

TM_FFN = 512
FF_CHUNK = 1408
TM_GLA = 512
GLA_CHUNK = 128
TM_KV = 512
TQ_FOX = 256
ADA_TN = 2048
SCAN_BS = 8
PAGES_PER_STEP = 8

N_MOD_BLOCKS = 2 * 9 + 2
MOD_ROWS = DEC_BATCH + SUBLANE
PROMPT_ROW_BLOCK = DEC_BATCH // SUBLANE


def _mod_col(layer, sub, kind):
    return layer * 9 + sub * 3 + kind


def _dot(a, b):
    return jnp.dot(a, b, preferred_element_type=F32)


def _dot_nt(a, b):
    return lax.dot_general(a, b, (((1,), (1,)), ((), ())), preferred_element_type=F32)


def _dot_tn(a, b):
    return lax.dot_general(a, b, (((0,), (0,)), ((), ())), preferred_element_type=F32)


def _split_dot(a_bf, x, terms):
    out = None
    r = x
    for _ in range(terms):
        p = r.astype(BF16)
        r = r - p.astype(F32)
        d = _dot(a_bf, p)
        out = d if out is None else out + d
    return out


def _split_dot_lhs(x, b_bf, terms):
    out = None
    r = x
    for _ in range(terms):
        p = r.astype(BF16)
        r = r - p.astype(F32)
        d = _dot(p, b_bf)
        out = d if out is None else out + d
    return out


def _log_sigmoid(x):
    return jnp.minimum(x, 0.0) - jnp.log1p(jnp.exp(-jnp.abs(x)))


def _rms(x):
    return x * lax.rsqrt(jnp.mean(x * x, axis=-1, keepdims=True) + NORM_EPS)


def _modulate(x, g, shift, scale):
    return (_rms(x) * g) * (1.0 + scale) + shift


def _get_mod(ref, prompt):
    if prompt:
        return ref[pl.ds(pl.program_id(0), 1), :]
    return ref[...]


def _lower_tri(n, strict=False):
    r = lax.broadcasted_iota(jnp.int32, (n, n), 0)
    c = lax.broadcasted_iota(jnp.int32, (n, n), 1)
    return (r > c) if strict else (r >= c)


def _head_indicator(transposed):
    if transposed:
        h = lax.broadcasted_iota(jnp.int32, (LANE, D_MODEL), 0)
        c = lax.broadcasted_iota(jnp.int32, (LANE, D_MODEL), 1)
    else:
        c = lax.broadcasted_iota(jnp.int32, (D_MODEL, LANE), 0)
        h = lax.broadcasted_iota(jnp.int32, (D_MODEL, LANE), 1)
    return jnp.where((c >> HD_SHIFT) == h, 1.0, 0.0).astype(BF16)


def _head_inv_rms(x):
    ss = _split_dot_lhs(x * x, _head_indicator(False), 2)
    inv = lax.rsqrt(ss * (1.0 / FOX_HD) + NORM_EPS)
    return _split_dot_lhs(inv, _head_indicator(True), 2)


class _Tok:
    def __init__(self, prompt, tm):
        self.prompt = prompt
        if prompt:
            self.grid = (BATCH, SEQ // tm)
            self.rows = (1, tm)
            self.mod_block = (SUBLANE, D_MODEL)
            self.mod_row = PROMPT_ROW_BLOCK
        else:
            self.grid = (1, 1)
            self.rows = (DEC_SEQ, DEC_BATCH)
            self.mod_block = (DEC_BATCH, D_MODEL)
            self.mod_row = 0

    def x(self, n, col=0):
        if self.prompt:
            return pl.BlockSpec(self.rows + (n,), lambda b, i: (b, i, col))
        return pl.BlockSpec(self.rows + (n,), lambda b, i: (0, 0, col))

    def mod(self, col):
        row = self.mod_row
        return pl.BlockSpec(self.mod_block, lambda b, i: (row, col))

    def shape(self, n):
        return (BATCH, SEQ, n) if self.prompt else (DEC_SEQ, DEC_BATCH, n)


def _resident(shape):
    nd = len(shape)
    return pl.BlockSpec(shape, lambda *_: (0,) * nd, pipeline_mode=pl.Buffered(1))


def _params():
    return pltpu.CompilerParams(dimension_semantics=("arbitrary", "arbitrary"), vmem_limit_bytes=VMEM_LIMIT)


def _ada_kernel(c_ref, w_ref, b_ref, o_ref):
    sc = jax.nn.silu(c_ref[...]).astype(BF16)
    o_ref[...] = _dot(sc, w_ref[...]) + b_ref[...]


def _ada(c_all, w_all, b_all):
    n = w_all.shape[1]
    return pl.pallas_call(
        _ada_kernel,
        grid=(n // ADA_TN,),
        in_specs=[pl.BlockSpec((MOD_ROWS, D_MODEL), lambda j: (0, 0)),
                  pl.BlockSpec((D_MODEL, ADA_TN), lambda j: (0, j)),
                  pl.BlockSpec((1, ADA_TN), lambda j: (0, j))],
        out_specs=pl.BlockSpec((MOD_ROWS, ADA_TN), lambda j: (0, j)),
        out_shape=jax.ShapeDtypeStruct((MOD_ROWS, n), F32),
        compiler_params=pltpu.CompilerParams(dimension_semantics=("arbitrary",), vmem_limit_bytes=VMEM_LIMIT),
        name="ada_mod",
    )(c_all, w_all, b_all)


def _ffn_kernel(x_ref, sh_ref, sc_ref, gt_ref, g_ref, wu_ref, wd_ref, o_ref, *, prompt):
    x = x_ref[...]
    g_, r_, _ = x.shape
    n = g_ * r_
    u = _modulate(x, g_ref[...], _get_mod(sh_ref, prompt), _get_mod(sc_ref, prompt))
    u = u.reshape(n, D_MODEL).astype(BF16)
    acc = None
    for c in range(D_FF // FF_CHUNK):
        lo = c * FF_CHUNK
        a = _dot(u, wu_ref[:, lo:lo + FF_CHUNK])
        b = _dot(u, wu_ref[:, D_FF + lo:D_FF + lo + FF_CHUNK])
        gated = (jax.nn.silu(a) * b).astype(BF16)
        part = _dot(gated, wd_ref[lo:lo + FF_CHUNK, :])
        acc = part if acc is None else acc + part
    o_ref[...] = x + (0.5 * _get_mod(gt_ref, prompt)) * acc.reshape(g_, r_, D_MODEL)


def _ffn(tok, x, mod_all, layer, sub, g, wu, wd, name):
    return pl.pallas_call(
        functools.partial(_ffn_kernel, prompt=tok.prompt),
        grid=tok.grid,
        in_specs=[tok.x(D_MODEL),
                  tok.mod(_mod_col(layer, sub, 0)), tok.mod(_mod_col(layer, sub, 1)), tok.mod(_mod_col(layer, sub, 2)),
                  _resident((1, D_MODEL)), _resident(wu.shape), _resident(wd.shape)],
        out_specs=tok.x(D_MODEL),
        out_shape=jax.ShapeDtypeStruct(x.shape, F32),
        compiler_params=_params(),
        name=name,
    )(x, mod_all, mod_all, mod_all, g, wu, wd)


def _gla_in(u, wqkvr_ref, wglr_ref, wg2_ref, bg_ref):
    proj = _dot(u, wqkvr_ref[...])
    glr = _dot(u, wglr_ref[...])
    xg = _dot(glr.astype(BF16), wg2_ref[...]) + bg_ref[...]
    return proj, _log_sigmoid(xg) * (1.0 / GLA_TAU)


def _gla_out(o, r, gout, wout_ref):
    heads = []
    for h in range(GLA_HEADS):
        oh = o[:, h * GLA_DV:(h + 1) * GLA_DV]
        heads.append(_rms(oh) * gout)
    y = (jnp.concatenate(heads, axis=-1) * jax.nn.silu(r)).astype(BF16)
    return _dot(y, wout_ref[...])


def _gla_prompt_kernel(x_ref, sh_ref, sc_ref, gt_ref, g_ref, wqkvr_ref, wglr_ref, wg2_ref, bg_ref, gout_ref,
                       wout_ref, s0_ref, o_ref, sout_ref, proj_scr, la_scr, oscan_scr, st_scr):
    i = pl.program_id(1)
    tm = x_ref.shape[1]
    x = x_ref[0]
    u = _modulate(x, g_ref[...], _get_mod(sh_ref, True), _get_mod(sc_ref, True)).astype(BF16)
    proj, log_a = _gla_in(u, wqkvr_ref, wglr_ref, wg2_ref, bg_ref)
    proj_scr[...] = proj
    la_scr[...] = log_a

    @pl.when(i == 0)
    def _():
        for h in range(GLA_HEADS):
            st_scr[h] = s0_ref[0, h].T

    tri = jnp.where(_lower_tri(GLA_CHUNK), 1.0, 0.0).astype(BF16)
    causal = _lower_tri(GLA_CHUNK)
    qscale = GLA_DK ** -0.5

    def chunk(c, carry):
        rows = pl.ds(pl.multiple_of(c * GLA_CHUNK, GLA_CHUNK), GLA_CHUNK)
        bcum = _split_dot(tri, la_scr[rows, :], 3)
        blast = bcum[GLA_CHUNK - 1:GLA_CHUNK, :]
        q = proj_scr[rows, 0:DK_ALL] * qscale
        k = proj_scr[rows, DK_ALL:2 * DK_ALL]
        qe = (q * jnp.exp(bcum)).astype(BF16)
        kinv = (k * jnp.exp(-bcum)).astype(BF16)
        kd = (k * jnp.exp(blast - bcum)).astype(BF16)
        elast = jnp.exp(blast)
        for h in range(GLA_HEADS):
            ks = slice(h * GLA_DK, (h + 1) * GLA_DK)
            vlo = 2 * DK_ALL + h * GLA_DV
            v = proj_scr[rows, vlo:vlo + GLA_DV].astype(BF16)
            st = st_scr[h]
            att = jnp.where(causal, _dot_nt(qe[:, ks], kinv[:, ks]), 0.0).astype(BF16)
            oscan_scr[rows, h * GLA_DV:(h + 1) * GLA_DV] = _dot_nt(qe[:, ks], st.astype(BF16)) + _dot(att, v)
            st_scr[h] = st * elast[:, ks] + _dot_tn(v, kd[:, ks])
        return carry

    lax.fori_loop(0, tm // GLA_CHUNK, chunk, 0)

    r = proj_scr[:, 2 * DK_ALL + DV_ALL:QKVR]
    mix = _gla_out(oscan_scr[...], r, gout_ref[...], wout_ref)
    o_ref[0] = x + _get_mod(gt_ref, True) * mix

    @pl.when(i == pl.num_programs(1) - 1)
    def _():
        for h in range(GLA_HEADS):
            sout_ref[0, h] = st_scr[h].T


def _gla_prompt(x, mod_all, g, wqkvr, wglr, wg2, bg, gout, wout, s0):
    tok = _Tok(True, TM_GLA)
    state_spec = pl.BlockSpec((1, GLA_HEADS, GLA_DK, GLA_DV), lambda b, i: (b, 0, 0, 0))
    return pl.pallas_call(
        _gla_prompt_kernel,
        grid=tok.grid,
        in_specs=[tok.x(D_MODEL), tok.mod(_mod_col(0, 1, 0)), tok.mod(_mod_col(0, 1, 1)), tok.mod(_mod_col(0, 1, 2)),
                  _resident((1, D_MODEL)), _resident(wqkvr.shape), _resident(wglr.shape), _resident(wg2.shape),
                  _resident(bg.shape), _resident(gout.shape), _resident(wout.shape), state_spec],
        out_specs=[tok.x(D_MODEL), state_spec],
        out_shape=[jax.ShapeDtypeStruct(x.shape, F32), jax.ShapeDtypeStruct(s0.shape, F32)],
        scratch_shapes=[pltpu.VMEM((TM_GLA, QKVR), F32), pltpu.VMEM((TM_GLA, DK_ALL), F32),
                        pltpu.VMEM((TM_GLA, DV_ALL), F32), pltpu.VMEM((GLA_HEADS, GLA_DV, GLA_DK), F32)],
        compiler_params=_params(),
        name="gla_prompt",
    )(x, mod_all, mod_all, mod_all, g, wqkvr, wglr, wg2, bg, gout, wout, s0)


def _gla_in_sample_kernel(x_ref, sh_ref, sc_ref, g_ref, wqkvr_ref, wglr_ref, wg2_ref, bg_ref, proj_ref, la_ref):
    x = x_ref[...]
    g_, r_, _ = x.shape
    u = _modulate(x, g_ref[...], _get_mod(sh_ref, False), _get_mod(sc_ref, False))
    u = u.reshape(g_ * r_, D_MODEL).astype(BF16)
    proj, log_a = _gla_in(u, wqkvr_ref, wglr_ref, wg2_ref, bg_ref)
    proj_ref[...] = proj.reshape(g_, r_, QKVR)
    la_ref[...] = log_a.reshape(g_, r_, DK_ALL)


def _gla_scan_sample_kernel(proj_ref, la_ref, s0_ref, o_ref, sout_ref, xt_scr):
    qscale = GLA_DK ** -0.5
    n_kind = DEC_SEQ * GLA_HEADS
    xt_scr[...] = jnp.zeros_like(xt_scr)

    def seq(j, carry):
        for t in range(DEC_SEQ):
            a_t = jnp.exp(la_ref[t, pl.ds(j, 1), :])
            q_t = proj_ref[t, pl.ds(j, 1), 0:DK_ALL] * qscale
            k_t = proj_ref[t, pl.ds(j, 1), DK_ALL:2 * DK_ALL]
            for h in range(GLA_HEADS):
                ks = slice(h * GLA_DK, (h + 1) * GLA_DK)
                row = h * DEC_SEQ + t
                xt_scr[row:row + 1, :] = a_t[:, ks]
                xt_scr[n_kind + row:n_kind + row + 1, :] = q_t[:, ks]
                xt_scr[2 * n_kind + row:2 * n_kind + row + 1, :] = k_t[:, ks]
        xt = xt_scr[...].T
        for h in range(GLA_HEADS):
            s = s0_ref[j, h]
            for t in range(DEC_SEQ):
                row = h * DEC_SEQ + t
                a_c = xt[:, row:row + 1]
                q_c = xt[:, n_kind + row:n_kind + row + 1]
                k_c = xt[:, 2 * n_kind + row:2 * n_kind + row + 1]
                vlo = 2 * DK_ALL + h * GLA_DV
                v_t = proj_ref[t, pl.ds(j, 1), vlo:vlo + GLA_DV]
                s = a_c * s + k_c * v_t
                o_ref[t, pl.ds(j, 1), h * GLA_DV:(h + 1) * GLA_DV] = jnp.sum(q_c * s, axis=0, keepdims=True)
            sout_ref[j, h] = s
        return carry

    lax.fori_loop(0, SCAN_BS, seq, 0)


def _gla_out_sample_kernel(x_ref, gt_ref, oscan_ref, r_ref, gout_ref, wout_ref, o_ref):
    x = x_ref[...]
    g_, r_, _ = x.shape
    n = g_ * r_
    mix = _gla_out(oscan_ref[...].reshape(n, DV_ALL), r_ref[...].reshape(n, DV_ALL), gout_ref[...], wout_ref)
    o_ref[...] = x + _get_mod(gt_ref, False) * mix.reshape(g_, r_, D_MODEL)


def _gla_sample(x, mod_all, g, wqkvr, wglr, wg2, bg, gout, wout, s0):
    tok = _Tok(False, 0)
    proj, log_a = pl.pallas_call(
        _gla_in_sample_kernel,
        grid=tok.grid,
        in_specs=[tok.x(D_MODEL), tok.mod(_mod_col(0, 1, 0)), tok.mod(_mod_col(0, 1, 1)),
                  _resident((1, D_MODEL)), _resident(wqkvr.shape), _resident(wglr.shape), _resident(wg2.shape),
                  _resident(bg.shape)],
        out_specs=[tok.x(QKVR), tok.x(DK_ALL)],
        out_shape=[jax.ShapeDtypeStruct(tok.shape(QKVR), F32), jax.ShapeDtypeStruct(tok.shape(DK_ALL), F32)],
        compiler_params=_params(),
        name="gla_in_sample",
    )(x, mod_all, mod_all, g, wqkvr, wglr, wg2, bg)

    state_spec = pl.BlockSpec((SCAN_BS, GLA_HEADS, GLA_DK, GLA_DV), lambda j: (j, 0, 0, 0))
    oscan, s_out = pl.pallas_call(
        _gla_scan_sample_kernel,
        grid=(DEC_BATCH // SCAN_BS,),
        in_specs=[pl.BlockSpec((DEC_SEQ, SCAN_BS, QKVR), lambda j: (0, j, 0)),
                  pl.BlockSpec((DEC_SEQ, SCAN_BS, DK_ALL), lambda j: (0, j, 0)),
                  state_spec],
        out_specs=[pl.BlockSpec((DEC_SEQ, SCAN_BS, DV_ALL), lambda j: (0, j, 0)), state_spec],
        out_shape=[jax.ShapeDtypeStruct(tok.shape(DV_ALL), F32), jax.ShapeDtypeStruct(s0.shape, F32)],
        scratch_shapes=[pltpu.VMEM((LANE, LANE), F32)],
        compiler_params=pltpu.CompilerParams(dimension_semantics=("arbitrary",), vmem_limit_bytes=VMEM_LIMIT),
        name="gla_scan_sample",
    )(proj, log_a, s0)

    h = pl.pallas_call(
        _gla_out_sample_kernel,
        grid=tok.grid,
        in_specs=[tok.x(D_MODEL), tok.mod(_mod_col(0, 1, 2)), tok.x(DV_ALL),
                  tok.x(DV_ALL, col=(2 * DK_ALL + DV_ALL) // DV_ALL), _resident(gout.shape), _resident(wout.shape)],
        out_specs=tok.x(D_MODEL),
        out_shape=jax.ShapeDtypeStruct(x.shape, F32),
        compiler_params=_params(),
        name="gla_out_sample",
    )(x, mod_all, oscan, proj, gout, wout)
    return h, s_out


def _kv_kernel(x_ref, sh_ref, sc_ref, g_ref, wkv_ref, wf_ref, bf_ref, gk_ref, *refs, prompt):
    if prompt:
        k_ref, v_ref, lf_ref, kb_ref, vb_ref, f_ref, ft_ref, carry_scr = refs
    else:
        k_ref, v_ref, lf_ref = refs
    x = x_ref[...]
    g_, r_, _ = x.shape
    n = g_ * r_
    u = _modulate(x, g_ref[...], _get_mod(sh_ref, prompt), _get_mod(sc_ref, prompt))
    u = u.reshape(n, D_MODEL).astype(BF16)
    kv = _dot(u, wkv_ref[...])
    k = kv[:, :D_MODEL]
    v = kv[:, D_MODEL:]
    kn = k * _head_inv_rms(k) * gk_ref[...]
    lf = _log_sigmoid(_dot(u, wf_ref[...]) + bf_ref[...])
    k_ref[...] = kn.reshape(g_, r_, D_MODEL)
    v_ref[...] = v.reshape(g_, r_, D_MODEL)
    lf_ref[...] = lf[:, :FOX_HEADS].reshape(g_, r_, FOX_HEADS)
    if prompt:
        kb_ref[...] = kn.astype(BF16).reshape(g_, r_, D_MODEL)
        vb_ref[...] = v.astype(BF16).reshape(g_, r_, D_MODEL)

        @pl.when(pl.program_id(1) == 0)
        def _():
            carry_scr[...] = jnp.zeros_like(carry_scr)

        tri = jnp.where(_lower_tri(n), 1.0, 0.0).astype(BF16)
        fsum = _split_dot(tri, lf, 3) + carry_scr[...]
        carry_scr[...] = fsum[n - 1:n, :]
        f_ref[...] = fsum[:, :FOX_HEADS].reshape(g_, r_, FOX_HEADS)
        ft_ref[0] = fsum.T[:FOX_HEADS, :]


def _kv(tok, x, mod_all, g, wkv, wf, bf, gk, name):
    prompt = tok.prompt
    out_specs = [tok.x(D_MODEL), tok.x(D_MODEL), tok.x(FOX_HEADS)]
    out_shape = [jax.ShapeDtypeStruct(tok.shape(D_MODEL), F32), jax.ShapeDtypeStruct(tok.shape(D_MODEL), F32),
                 jax.ShapeDtypeStruct(tok.shape(FOX_HEADS), F32)]
    scratch = []
    if prompt:
        out_specs += [tok.x(D_MODEL), tok.x(D_MODEL), tok.x(FOX_HEADS),
                      pl.BlockSpec((1, FOX_HEADS, TM_KV), lambda b, i: (b, 0, i))]
        out_shape += [jax.ShapeDtypeStruct(tok.shape(D_MODEL), BF16), jax.ShapeDtypeStruct(tok.shape(D_MODEL), BF16),
                      jax.ShapeDtypeStruct(tok.shape(FOX_HEADS), F32),
                      jax.ShapeDtypeStruct((BATCH, FOX_HEADS, SEQ), F32)]
        scratch = [pltpu.VMEM((1, LANE), F32)]
    return pl.pallas_call(
        functools.partial(_kv_kernel, prompt=prompt),
        grid=tok.grid,
        in_specs=[tok.x(D_MODEL), tok.mod(18), tok.mod(19), _resident((1, D_MODEL)), _resident(wkv.shape),
                  _resident(wf.shape), _resident(bf.shape), _resident(gk.shape)],
        out_specs=out_specs,
        out_shape=out_shape,
        scratch_shapes=scratch,
        compiler_params=_params(),
        name=name,
    )(x, mod_all, mod_all, g, wkv, wf, bf, gk)


def _fox_q(u, wqg_ref, gq_ref):
    qg = _dot(u, wqg_ref[...])
    q = qg[:, :D_MODEL]
    og = qg[:, D_MODEL:]
    qn = q * _head_inv_rms(q) * gq_ref[...] * (FOX_HD ** -0.5)
    return qn, og


def _softmax_step(s, m, l, acc, v):
    m_new = jnp.maximum(m, jnp.max(s, axis=-1, keepdims=True))
    alpha = jnp.exp(m - m_new)
    p = jnp.exp(s - m_new)
    l_new = alpha * l + jnp.sum(p, axis=-1, keepdims=True)
    acc_new = alpha * acc + _dot(p.astype(BF16), v)
    return m_new, l_new, acc_new


def _fox_prompt_kernel(x_ref, sh_ref, sc_ref, gt_ref, g_ref, wqg_ref, gq_ref, wo_ref, kb_ref, vb_ref, f_ref, ft_ref,
                       o_ref, qn_scr, og_scr, oatt_scr):
    i = pl.program_id(1)
    tq = x_ref.shape[1]
    x = x_ref[0]
    u = _modulate(x, g_ref[...], _get_mod(sh_ref, True), _get_mod(sc_ref, True)).astype(BF16)
    qn, og = _fox_q(u, wqg_ref, gq_ref)
    qn_scr[...] = qn.astype(BF16)
    og_scr[...] = og
    fq_all = f_ref[0]
    lane = lax.broadcasted_iota(jnp.int32, (tq, LANE), 1)
    first = lane < FOX_HD
    causal = _lower_tri(tq)
    zero = jnp.zeros((), BF16)

    for pair in range(FOX_HEADS // 2):
        cols = slice(pair * LANE, (pair + 1) * LANE)
        q2 = qn_scr[:, cols]
        qs = (jnp.where(first, q2, zero), jnp.where(first, zero, q2))
        fqs = (fq_all[:, 2 * pair:2 * pair + 1], fq_all[:, 2 * pair + 1:2 * pair + 2])

        def step(j, carry, diagonal):
            rows = pl.ds(pl.multiple_of(j * tq, tq), tq)
            k2 = kb_ref[0, rows, cols]
            v2 = vb_ref[0, rows, cols]
            out = []
            for hh in range(2):
                m, l, acc = carry[hh]
                fk = ft_ref[0, 2 * pair + hh, pl.ds(j, 1), :]
                s = _dot_nt(qs[hh], k2) + (fqs[hh] - fk)
                if diagonal:
                    s = jnp.where(causal, s, NEG)
                out.append(_softmax_step(s, m, l, acc, v2))
            return tuple(out)

        init = tuple((jnp.full((tq, 1), NEG, F32), jnp.zeros((tq, 1), F32), jnp.zeros((tq, LANE), F32))
                     for _ in range(2))
        carry = lax.fori_loop(0, i, functools.partial(step, diagonal=False), init)
        (_, l0, a0), (_, l1, a1) = step(i, carry, True)
        oatt_scr[:, cols] = jnp.where(first, a0 / l0, a1 / l1)

    gated = (oatt_scr[...] * jax.nn.sigmoid(og_scr[...])).astype(BF16)
    o_ref[0] = x + _get_mod(gt_ref, True) * _dot(gated, wo_ref[...])


def _fox_prompt(x, mod_all, g, wqg, gq, wo, kb, vb, fsum, fsum_t):
    tok = _Tok(True, TQ_FOX)
    nk = SEQ // TQ_FOX
    seq_spec = pl.BlockSpec((1, SEQ, D_MODEL), lambda b, i: (b, 0, 0), pipeline_mode=pl.Buffered(1))
    return pl.pallas_call(
        _fox_prompt_kernel,
        grid=tok.grid,
        in_specs=[tok.x(D_MODEL), tok.mod(_mod_col(1, 1, 0)), tok.mod(_mod_col(1, 1, 1)), tok.mod(_mod_col(1, 1, 2)),
                  _resident((1, D_MODEL)), _resident(wqg.shape), _resident(gq.shape), _resident(wo.shape),
                  seq_spec, seq_spec, tok.x(FOX_HEADS),
                  pl.BlockSpec((1, FOX_HEADS, nk, TQ_FOX), lambda b, i: (b, 0, 0, 0))],
        out_specs=tok.x(D_MODEL),
        out_shape=jax.ShapeDtypeStruct(x.shape, F32),
        scratch_shapes=[pltpu.VMEM((TQ_FOX, D_MODEL), BF16), pltpu.VMEM((TQ_FOX, D_MODEL), F32),
                        pltpu.VMEM((TQ_FOX, D_MODEL), F32)],
        compiler_params=_params(),
        name="fox_prompt",
    )(x, mod_all, mod_all, mod_all, g, wqg, gq, wo, kb, vb, fsum, fsum_t.reshape(BATCH, FOX_HEADS, nk, TQ_FOX))


def _fox_q_sample_kernel(x_ref, sh_ref, sc_ref, g_ref, wqg_ref, gq_ref, q_ref, og_ref):
    x = x_ref[...]
    g_, r_, _ = x.shape
    u = _modulate(x, g_ref[...], _get_mod(sh_ref, False), _get_mod(sc_ref, False))
    qn, og = _fox_q(u.reshape(g_ * r_, D_MODEL).astype(BF16), wqg_ref, gq_ref)
    q_ref[...] = qn.reshape(g_, r_, D_MODEL)
    og_ref[...] = og.reshape(g_, r_, D_MODEL)


def _fox_out_sample_kernel(x_ref, gt_ref, oatt_ref, og_ref, wo_ref, o_ref):
    x = x_ref[...]
    g_, r_, _ = x.shape
    n = g_ * r_
    gated = (oatt_ref[...] * jax.nn.sigmoid(og_ref[...])).reshape(n, D_MODEL).astype(BF16)
    o_ref[...] = x + _get_mod(gt_ref, False) * _dot(gated, wo_ref[...]).reshape(g_, r_, D_MODEL)


def _paged_kernel(pt_ref, q_ref, kn_ref, vn_ref, lnt_ref, *refs):
    del pt_ref
    npg = PAGES_PER_STEP
    k_pages = refs[:npg]
    v_pages = refs[npg:2 * npg]
    l_pages = refs[2 * npg:3 * npg]
    o_ref = refs[3 * npg]
    qbd_scr, m_scr, l_scr, acc_scr, tot_scr, kn_scr, vn_scr = refs[3 * npg + 1:]
    g = pl.program_id(1)
    nrow = DEC_SEQ * FOX_HEADS

    hrow = lax.broadcasted_iota(jnp.int32, (FOX_HEADS, D_MODEL), 0)
    hcol = lax.broadcasted_iota(jnp.int32, (FOX_HEADS, D_MODEL), 1)
    head_mask = (hcol >> HD_SHIFT) == hrow

    @pl.when(g == 0)
    def _():
        q = q_ref[0]
        blocks = [jnp.where(head_mask, jnp.broadcast_to(q[t:t + 1, :], (FOX_HEADS, D_MODEL)), 0.0)
                  for t in range(DEC_SEQ)]
        qbd_scr[...] = jnp.concatenate(blocks, axis=0).astype(BF16)
        m_scr[...] = jnp.full_like(m_scr, NEG)
        l_scr[...] = jnp.zeros_like(l_scr)
        acc_scr[...] = jnp.zeros_like(acc_scr)
        tot_scr[...] = jnp.zeros_like(tot_scr)

    lnt = lnt_ref[0]
    cn = [lnt[:, 0:1]]
    for t in range(1, DEC_SEQ):
        cn.append(cn[-1] + lnt[:, t:t + 1])
    cn_col = jnp.concatenate(cn, axis=0)
    qbd = qbd_scr[...]

    ri = lax.broadcasted_iota(jnp.int32, (PAGE_SIZE, 2 * LANE), 0)
    ci = lax.broadcasted_iota(jnp.int32, (PAGE_SIZE, 2 * LANE), 1)
    suffix = jnp.where(((ci < PAGE_SIZE) & (ri > ci)) | (ci == PAGE_SIZE), 1.0, 0.0).astype(BF16)

    def update(s, v):
        m, l, acc = _softmax_step(s, m_scr[...], l_scr[...], acc_scr[...], v)
        m_scr[...] = m
        l_scr[...] = l
        acc_scr[...] = acc

    for idx in reversed(range(npg)):
        sums = _split_dot_lhs(l_pages[idx][0], suffix, 2)
        tot = tot_scr[...]
        bias16 = sums[:, :PAGE_SIZE] + tot
        bias = jnp.concatenate([bias16] * DEC_SEQ, axis=0) + cn_col
        s = _dot_nt(qbd, k_pages[idx][0].astype(BF16)) + bias
        update(s, v_pages[idx][0].astype(BF16))
        tot_scr[...] = tot + sums[:, PAGE_SIZE:PAGE_SIZE + 1]

    @pl.when(g == pl.num_programs(1) - 1)
    def _():
        kn_scr[...] = jnp.zeros_like(kn_scr)
        vn_scr[...] = jnp.zeros_like(vn_scr)
        kn_scr[0:DEC_SEQ, :] = kn_ref[0]
        vn_scr[0:DEC_SEQ, :] = vn_ref[0]
        lane = lax.broadcasted_iota(jnp.int32, (nrow, LANE), 1)
        step_of_row = lax.broadcasted_iota(jnp.int32, (nrow, LANE), 0) >> HEAD_SHIFT
        cn_keys = jnp.zeros((nrow, LANE), F32)
        for t in range(DEC_SEQ):
            cn_keys = jnp.where(lane == t, jnp.concatenate([cn[t]] * DEC_SEQ, axis=0), cn_keys)
        s = _dot_nt(qbd, kn_scr[...].astype(BF16)) + (cn_col - cn_keys)
        s = jnp.where(lane <= step_of_row, s, NEG)
        update(s, vn_scr[...].astype(BF16))
        out = acc_scr[...] / l_scr[...]
        for t in range(DEC_SEQ):
            blk = out[t * FOX_HEADS:(t + 1) * FOX_HEADS, :]
            o_ref[0, t:t + 1, :] = jnp.sum(jnp.where(head_mask, blk, 0.0), axis=0, keepdims=True)


def _paged_attend(q, k_new, v_new, lf_new_t, cache_k, cache_v, cache_lf_t, page_table):
    npg = PAGES_PER_STEP
    ng = N_PAGES // npg
    nrow = DEC_SEQ * FOX_HEADS

    def page_spec(shape, idx):
        return pl.BlockSpec((1,) + shape, lambda b, g, pt: (pt[b, (ng - 1 - g) * npg + idx], 0, 0))

    seq_spec = pl.BlockSpec((1, DEC_SEQ, D_MODEL), lambda b, g, pt: (b, 0, 0))
    in_specs = [seq_spec, seq_spec, seq_spec, pl.BlockSpec((1, FOX_HEADS, DEC_SEQ), lambda b, g, pt: (b, 0, 0))]
    in_specs += [page_spec((PAGE_SIZE, D_MODEL), i) for i in range(npg)]
    in_specs += [page_spec((PAGE_SIZE, D_MODEL), i) for i in range(npg)]
    in_specs += [page_spec((FOX_HEADS, PAGE_SIZE), i) for i in range(npg)]
    grid_spec = pltpu.PrefetchScalarGridSpec(
        num_scalar_prefetch=1,
        grid=(DEC_BATCH, ng),
        in_specs=in_specs,
        out_specs=seq_spec,
        scratch_shapes=[pltpu.VMEM((nrow, D_MODEL), BF16), pltpu.VMEM((nrow, 1), F32), pltpu.VMEM((nrow, 1), F32),
                        pltpu.VMEM((nrow, D_MODEL), F32), pltpu.VMEM((FOX_HEADS, 1), F32),
                        pltpu.VMEM((PAGE_SIZE, D_MODEL), F32), pltpu.VMEM((PAGE_SIZE, D_MODEL), F32)],
    )
    return pl.pallas_call(
        _paged_kernel,
        grid_spec=grid_spec,
        out_shape=jax.ShapeDtypeStruct((DEC_BATCH, DEC_SEQ, D_MODEL), F32),
        compiler_params=_params(),
        name="fox_paged",
    )(page_table, q, k_new, v_new, lf_new_t, *([cache_k] * npg), *([cache_v] * npg), *([cache_lf_t] * npg))


def _fox_sample(x, mod_all, g, wqg, gq, wo, k_new, v_new, lf_new, cache_k, cache_v, cache_logf, page_table):
    tok = _Tok(False, 0)
    qn, og = pl.pallas_call(
        _fox_q_sample_kernel,
        grid=tok.grid,
        in_specs=[tok.x(D_MODEL), tok.mod(_mod_col(1, 1, 0)), tok.mod(_mod_col(1, 1, 1)), _resident((1, D_MODEL)),
                  _resident(wqg.shape), _resident(gq.shape)],
        out_specs=[tok.x(D_MODEL), tok.x(D_MODEL)],
        out_shape=[jax.ShapeDtypeStruct(x.shape, F32), jax.ShapeDtypeStruct(x.shape, F32)],
        compiler_params=_params(),
        name="fox_q_sample",
    )(x, mod_all, mod_all, g, wqg, gq)

    n_phys = cache_k.shape[0]
    oatt = _paged_attend(
        qn.transpose(1, 0, 2), k_new.transpose(1, 0, 2), v_new.transpose(1, 0, 2), lf_new.transpose(1, 2, 0),
        cache_k.reshape(n_phys, PAGE_SIZE, D_MODEL), cache_v.reshape(n_phys, PAGE_SIZE, D_MODEL),
        cache_logf.transpose(0, 2, 1), page_table)

    return pl.pallas_call(
        _fox_out_sample_kernel,
        grid=tok.grid,
        in_specs=[tok.x(D_MODEL), tok.mod(_mod_col(1, 1, 2)), tok.x(D_MODEL), tok.x(D_MODEL), _resident(wo.shape)],
        out_specs=tok.x(D_MODEL),
        out_shape=jax.ShapeDtypeStruct(x.shape, F32),
        compiler_params=_params(),
        name="fox_out_sample",
    )(x, mod_all, oatt.transpose(1, 0, 2), og, wo)


def kernel(x_prompt, x_sample, state_gla, cache_k, cache_v, cache_logf, page_table, c_prompt, c_sample, w_ada, b_ada, g_norm, w_ffn_up, w_ffn_down, gla_w_in, gla_w_gate2, gla_b_gate, gla_g_out, gla_w_out, w_ada_kv, b_ada_kv, g_kv, w_kvf, b_f, g_k, fox_w_qg, fox_g_q, fox_w_o):
    w_mod = jnp.concatenate([w_ada[0], w_ada[1], w_ada_kv], axis=1).astype(BF16)
    b_mod = jnp.concatenate([b_ada[0], b_ada[1], b_ada_kv], axis=0)[None, :]
    c_all = jnp.concatenate([c_sample, c_prompt, jnp.zeros((MOD_ROWS - DEC_BATCH - BATCH, D_MODEL), F32)], axis=0)
    wu = w_ffn_up.astype(BF16)
    wd = w_ffn_down.astype(BF16)
    w_in = gla_w_in[0]
    wqkvr = w_in[:, :QKVR].astype(BF16)
    wglr = jnp.pad(w_in[:, QKVR:], ((0, 0), (0, LANE - GLA_RANK))).astype(BF16)
    wg2 = jnp.pad(gla_w_gate2[0], ((0, LANE - GLA_RANK), (0, 0))).astype(BF16)
    bg = gla_b_gate[0][None, :]
    gout = gla_g_out[0][None, :]
    wout = gla_w_out[0].astype(BF16)
    wkv = w_kvf[:, :2 * D_MODEL].astype(BF16)
    wf = jnp.pad(w_kvf[:, 2 * D_MODEL:], ((0, 0), (0, LANE - FOX_HEADS))).astype(BF16)
    bf = jnp.pad(b_f, (0, LANE - FOX_HEADS))[None, :]
    gk = jnp.tile(g_k, FOX_HEADS)[None, :]
    wqg = fox_w_qg[0].astype(BF16)
    gq = jnp.tile(fox_g_q[0], FOX_HEADS)[None, :]
    wo = fox_w_o[0].astype(BF16)
    gkv = g_kv[None, :]

    def gn(layer, sub):
        return g_norm[layer, sub][None, :]

    mod_all = _ada(c_all, w_mod, b_mod)

    tok = _Tok(True, TM_FFN)
    h = _ffn(tok, x_prompt, mod_all, 0, 0, gn(0, 0), wu[0, 0], wd[0, 0], "ffn_p00")
    s0 = jnp.zeros((BATCH, GLA_HEADS, GLA_DK, GLA_DV), F32)
    h, sg_prompt = _gla_prompt(h, mod_all, gn(0, 1), wqkvr, wglr, wg2, bg, gout, wout, s0)
    h = _ffn(tok, h, mod_all, 0, 2, gn(0, 2), wu[0, 1], wd[0, 1], "ffn_p02")
    k_p, v_p, lf_p, kb, vb, fsum, fsum_t = _kv(_Tok(True, TM_KV), h, mod_all, gkv, wkv, wf, bf, gk, "kv_prompt")
    h = _ffn(tok, h, mod_all, 1, 0, gn(1, 0), wu[1, 0], wd[1, 0], "ffn_p10")
    h = _fox_prompt(h, mod_all, gn(1, 1), wqg, gq, wo, kb, vb, fsum, fsum_t)
    y_prompt = _ffn(tok, h, mod_all, 1, 2, gn(1, 2), wu[1, 1], wd[1, 1], "ffn_p12")

    tok = _Tok(False, 0)
    hs = x_sample.transpose(1, 0, 2)
    hs = _ffn(tok, hs, mod_all, 0, 0, gn(0, 0), wu[0, 0], wd[0, 0], "ffn_s00")
    hs, sg_sample = _gla_sample(hs, mod_all, gn(0, 1), wqkvr, wglr, wg2, bg, gout, wout, state_gla[0])
    hs = _ffn(tok, hs, mod_all, 0, 2, gn(0, 2), wu[0, 1], wd[0, 1], "ffn_s02")
    k_s, v_s, lf_s = _kv(tok, hs, mod_all, gkv, wkv, wf, bf, gk, "kv_sample")
    hs = _ffn(tok, hs, mod_all, 1, 0, gn(1, 0), wu[1, 0], wd[1, 0], "ffn_s10")
    hs = _fox_sample(hs, mod_all, gn(1, 1), wqg, gq, wo, k_s, v_s, lf_s, cache_k, cache_v, cache_logf, page_table)
    hs = _ffn(tok, hs, mod_all, 1, 2, gn(1, 2), wu[1, 1], wd[1, 1], "ffn_s12")
    y_sample = hs.transpose(1, 0, 2)

    def heads(t, lead):
        return t.reshape(lead + (FOX_HEADS, FOX_HD))

    return (y_prompt, y_sample, sg_prompt[None],
            heads(k_p, (BATCH, SEQ)), heads(v_p, (BATCH, SEQ)), lf_p,
            sg_sample[None],
            heads(k_s.transpose(1, 0, 2), (DEC_BATCH, DEC_SEQ)), heads(v_s.transpose(1, 0, 2), (DEC_BATCH, DEC_SEQ)),
            lf_s.transpose(1, 0, 2))
```

```python
import functools

import jax
import jax.numpy as jnp
from jax import lax
from jax.experimental import pallas as pl
from jax.experimental.pallas import tpu as pltpu

F32 = jnp.float32
BF16 = jnp.bfloat16

D_MODEL = 1024
BATCH = 4
SEQ = 4096
DEC_BATCH = 128
DEC_SEQ = 4
PAST_LEN = 2048
PAGE_SIZE = 128
N_PAGES = PAST_LEN // PAGE_SIZE
GLA_HEADS = 4
GLA_DK = 128
GLA_DV = 256
GLA_RANK = 16
GLA_TAU = 16.0
FOX_HEADS = 16
FOX_HD = 64
D_FF = 2816
NORM_EPS = 1e-6
HD_SHIFT = FOX_HD.bit_length() - 1
HEAD_SHIFT = FOX_HEADS.bit_length() - 1

DK_ALL = GLA_HEADS * GLA_DK
DV_ALL = GLA_HEADS * GLA_DV
QKVR = 2 * DK_ALL + 2 * DV_ALL
LANE = 128
SUBLANE = 8

NEG = -1e30
LOG2E = 1.4426950408889634
VMEM_LIMIT = 56 * 1024 * 1024

TM_FFN = 512
FF_CHUNK = 1408
TM_GLA = 512
GLA_CHUNK = 128
TM_KV = 512
TQ_FOX = 512
FOX_GROUP = 4
ADA_TN = 2048
SCAN_BS = 8

N_MOD_BLOCKS = 2 * 9 + 2
MOD_ROWS = DEC_BATCH + SUBLANE
PROMPT_ROW_BLOCK = DEC_BATCH // SUBLANE


def _mod_col(layer, sub, kind):
    return layer * 9 + sub * 3 + kind


def _dot(a, b):
    return jnp.dot(a, b, preferred_element_type=F32)


def _dot_nt(a, b):
    return lax.dot_general(a, b, (((1,), (1,)), ((), ())), preferred_element_type=F32)


def _dot_tn(a, b):
    return lax.dot_general(a, b, (((0,), (0,)), ((), ())), preferred_element_type=F32)


def _split_dot(a_bf, x, terms):
    out = None
    r = x
    for _ in range(terms):
        p = r.astype(BF16)
        r = r - p.astype(F32)
        d = _dot(a_bf, p)
        out = d if out is None else out + d
    return out


def _split_dot_lhs(x, b_bf, terms):
    out = None
    r = x
    for _ in range(terms):
        p = r.astype(BF16)
        r = r - p.astype(F32)
        d = _dot(p, b_bf)
        out = d if out is None else out + d
    return out


def _log_sigmoid(x):
    return jnp.minimum(x, 0.0) - jnp.log1p(jnp.exp(-jnp.abs(x)))


def _rms(x):
    return x * lax.rsqrt(jnp.mean(x * x, axis=-1, keepdims=True) + NORM_EPS)


def _modulate(x, g, shift, scale):
    return (_rms(x) * g) * (1.0 + scale) + shift


def _get_mod(ref, prompt):
    if prompt:
        return ref[pl.ds(pl.program_id(0), 1), :]
    return ref[...]


def _lower_tri(n, strict=False):
    r = lax.broadcasted_iota(jnp.int32, (n, n), 0)
    c = lax.broadcasted_iota(jnp.int32, (n, n), 1)
    return (r > c) if strict else (r >= c)


def _head_indicator(transposed):
    if transposed:
        h = lax.broadcasted_iota(jnp.int32, (LANE, D_MODEL), 0)
        c = lax.broadcasted_iota(jnp.int32, (LANE, D_MODEL), 1)
    else:
        c = lax.broadcasted_iota(jnp.int32, (D_MODEL, LANE), 0)
        h = lax.broadcasted_iota(jnp.int32, (D_MODEL, LANE), 1)
    return jnp.where((c >> HD_SHIFT) == h, 1.0, 0.0).astype(BF16)


def _head_inv_rms(x):
    ss = _split_dot_lhs(x * x, _head_indicator(False), 2)
    inv = lax.rsqrt(ss * (1.0 / FOX_HD) + NORM_EPS)
    return _split_dot_lhs(inv, _head_indicator(True), 2)


class _Tok:
    def __init__(self, prompt, tm):
        self.prompt = prompt
        if prompt:
            self.grid = (BATCH, SEQ // tm)
            self.rows = (1, tm)
            self.mod_block = (SUBLANE, D_MODEL)
            self.mod_row = PROMPT_ROW_BLOCK
        else:
            self.grid = (1, 1)
            self.rows = (DEC_SEQ, DEC_BATCH)
            self.mod_block = (DEC_BATCH, D_MODEL)
            self.mod_row = 0

    def x(self, n, col=0):
        if self.prompt:
            return pl.BlockSpec(self.rows + (n,), lambda b, i: (b, i, col))
        return pl.BlockSpec(self.rows + (n,), lambda b, i: (0, 0, col))

    def mod(self, col):
        row = self.mod_row
        return pl.BlockSpec(self.mod_block, lambda b, i: (row, col))

    def shape(self, n):
        return (BATCH, SEQ, n) if self.prompt else (DEC_SEQ, DEC_BATCH, n)


def _resident(shape):
    nd = len(shape)
    return pl.BlockSpec(shape, lambda *_: (0,) * nd, pipeline_mode=pl.Buffered(1))


def _resident_at(shape, lead):
    nd = len(shape) - len(lead)
    block = (None,) * len(lead) + tuple(shape[len(lead):])
    return pl.BlockSpec(block, lambda *_: tuple(lead) + (0,) * nd, pipeline_mode=pl.Buffered(1))


def _params():
    return pltpu.CompilerParams(dimension_semantics=("arbitrary", "arbitrary"), vmem_limit_bytes=VMEM_LIMIT)


def _ada_kernel(c_ref, w_ref, b_ref, o_ref):
    sc = jax.nn.silu(c_ref[...]).astype(BF16)
    o_ref[...] = _dot(sc, w_ref[...]) + b_ref[...]


def _ada(c_all, w_all, b_all):
    n = w_all.shape[1]
    return pl.pallas_call(
        _ada_kernel,
        grid=(n // ADA_TN,),
        in_specs=[pl.BlockSpec((MOD_ROWS, D_MODEL), lambda j: (0, 0)),
                  pl.BlockSpec((D_MODEL, ADA_TN), lambda j: (0, j)),
                  pl.BlockSpec((1, ADA_TN), lambda j: (0, j))],
        out_specs=pl.BlockSpec((MOD_ROWS, ADA_TN), lambda j: (0, j)),
        out_shape=jax.ShapeDtypeStruct((MOD_ROWS, n), F32),
        compiler_params=pltpu.CompilerParams(dimension_semantics=("arbitrary",), vmem_limit_bytes=VMEM_LIMIT),
        name="ada_mod",
    )(c_all, w_all, b_all)


def _ffn_kernel(x_ref, sh_ref, sc_ref, gt_ref, g_ref, wu_ref, wd_ref, o_ref, *, prompt):
    x = x_ref[...]
    g_, r_, _ = x.shape
    n = g_ * r_
    u = _modulate(x, g_ref[...], _get_mod(sh_ref, prompt), _get_mod(sc_ref, prompt))
    u = u.reshape(n, D_MODEL).astype(BF16)
    acc = None
    for c in range(D_FF // FF_CHUNK):
        lo = c * FF_CHUNK
        a = _dot(u, wu_ref[:, lo:lo + FF_CHUNK])
        b = _dot(u, wu_ref[:, D_FF + lo:D_FF + lo + FF_CHUNK])
        gated = (jax.nn.silu(a) * b).astype(BF16)
        part = _dot(gated, wd_ref[lo:lo + FF_CHUNK, :])
        acc = part if acc is None else acc + part
    o_ref[...] = x + (0.5 * _get_mod(gt_ref, prompt)) * acc.reshape(g_, r_, D_MODEL)


def _ffn(tok, x, mod_all, layer, sub, g, wu, wd, name):
    which = (layer, sub // 2)
    return pl.pallas_call(
        functools.partial(_ffn_kernel, prompt=tok.prompt),
        grid=tok.grid,
        in_specs=[tok.x(D_MODEL),
                  tok.mod(_mod_col(layer, sub, 0)), tok.mod(_mod_col(layer, sub, 1)), tok.mod(_mod_col(layer, sub, 2)),
                  _resident((1, D_MODEL)), _resident_at(wu.shape, which), _resident_at(wd.shape, which)],
        out_specs=tok.x(D_MODEL),
        out_shape=jax.ShapeDtypeStruct(x.shape, F32),
        compiler_params=_params(),
        name=name,
    )(x, mod_all, mod_all, mod_all, g, wu, wd)


def _gla_in(u, wqkvr_ref, wglr_ref, wg2_ref, bg_ref):
    proj = _dot(u, wqkvr_ref[...])
    glr = _dot(u, wglr_ref[...])
    xg = _dot(glr.astype(BF16), wg2_ref[...]) + bg_ref[...]
    return proj, _log_sigmoid(xg) * (1.0 / GLA_TAU)


def _gla_out(o, r, gout, wout_ref):
    heads = []
    for h in range(GLA_HEADS):
        oh = o[:, h * GLA_DV:(h + 1) * GLA_DV]
        heads.append(_rms(oh) * gout)
    y = (jnp.concatenate(heads, axis=-1) * jax.nn.silu(r)).astype(BF16)
    return _dot(y, wout_ref[...])


def _gla_prompt_kernel(x_ref, sh_ref, sc_ref, gt_ref, g_ref, wqkvr_ref, wglr_ref, wg2_ref, bg_ref, gout_ref,
                       wout_ref, s0_ref, o_ref, sout_ref, proj_scr, la_scr, oscan_scr, st_scr):
    i = pl.program_id(1)
    tm = x_ref.shape[1]
    x = x_ref[0]
    u = _modulate(x, g_ref[...], _get_mod(sh_ref, True), _get_mod(sc_ref, True)).astype(BF16)
    proj, log_a = _gla_in(u, wqkvr_ref, wglr_ref, wg2_ref, bg_ref)
    proj_scr[...] = proj
    la_scr[...] = log_a

    @pl.when(i == 0)
    def _():
        for h in range(GLA_HEADS):
            st_scr[h] = s0_ref[0, h].T

    tri = jnp.where(_lower_tri(GLA_CHUNK), 1.0, 0.0).astype(BF16)
    causal = _lower_tri(GLA_CHUNK)
    qscale = GLA_DK ** -0.5

    def chunk(c, carry):
        rows = pl.ds(pl.multiple_of(c * GLA_CHUNK, GLA_CHUNK), GLA_CHUNK)
        bcum = _split_dot(tri, la_scr[rows, :], 3)
        blast = bcum[GLA_CHUNK - 1:GLA_CHUNK, :]
        q = proj_scr[rows, 0:DK_ALL] * qscale
        k = proj_scr[rows, DK_ALL:2 * DK_ALL]
        qe = (q * jnp.exp(bcum)).astype(BF16)
        kinv = (k * jnp.exp(-bcum)).astype(BF16)
        kd = (k * jnp.exp(blast - bcum)).astype(BF16)
        elast = jnp.exp(blast)
        for h in range(GLA_HEADS):
            ks = slice(h * GLA_DK, (h + 1) * GLA_DK)
            vlo = 2 * DK_ALL + h * GLA_DV
            v = proj_scr[rows, vlo:vlo + GLA_DV].astype(BF16)
            st = st_scr[h]
            att = jnp.where(causal, _dot_nt(qe[:, ks], kinv[:, ks]), 0.0).astype(BF16)
            oscan_scr[rows, h * GLA_DV:(h + 1) * GLA_DV] = _dot_nt(qe[:, ks], st.astype(BF16)) + _dot(att, v)
            st_scr[h] = st * elast[:, ks] + _dot_tn(v, kd[:, ks])
        return carry

    lax.fori_loop(0, tm // GLA_CHUNK, chunk, 0)

    r = proj_scr[:, 2 * DK_ALL + DV_ALL:QKVR]
    mix = _gla_out(oscan_scr[...], r, gout_ref[...], wout_ref)
    o_ref[0] = x + _get_mod(gt_ref, True) * mix

    @pl.when(i == pl.num_programs(1) - 1)
    def _():
        for h in range(GLA_HEADS):
            sout_ref[0, h] = st_scr[h].T


def _gla_prompt(x, mod_all, g, wqkvr, wglr, wg2, bg, gout, wout, s0):
    tok = _Tok(True, TM_GLA)
    state_spec = pl.BlockSpec((1, GLA_HEADS, GLA_DK, GLA_DV), lambda b, i: (b, 0, 0, 0))
    return pl.pallas_call(
        _gla_prompt_kernel,
        grid=tok.grid,
        in_specs=[tok.x(D_MODEL), tok.mod(_mod_col(0, 1, 0)), tok.mod(_mod_col(0, 1, 1)), tok.mod(_mod_col(0, 1, 2)),
                  _resident((1, D_MODEL)), _resident(wqkvr.shape), _resident(wglr.shape), _resident(wg2.shape),
                  _resident(bg.shape), _resident(gout.shape), _resident(wout.shape), state_spec],
        out_specs=[tok.x(D_MODEL), state_spec],
        out_shape=[jax.ShapeDtypeStruct(x.shape, F32), jax.ShapeDtypeStruct(s0.shape, F32)],
        scratch_shapes=[pltpu.VMEM((TM_GLA, QKVR), F32), pltpu.VMEM((TM_GLA, DK_ALL), F32),
                        pltpu.VMEM((TM_GLA, DV_ALL), F32), pltpu.VMEM((GLA_HEADS, GLA_DV, GLA_DK), F32)],
        compiler_params=_params(),
        name="gla_prompt",
    )(x, mod_all, mod_all, mod_all, g, wqkvr, wglr, wg2, bg, gout, wout, s0)


def _gla_in_sample_kernel(x_ref, sh_ref, sc_ref, g_ref, wqkvr_ref, wglr_ref, wg2_ref, bg_ref, proj_ref, la_ref):
    x = x_ref[...]
    g_, r_, _ = x.shape
    u = _modulate(x, g_ref[...], _get_mod(sh_ref, False), _get_mod(sc_ref, False))
    u = u.reshape(g_ * r_, D_MODEL).astype(BF16)
    proj, log_a = _gla_in(u, wqkvr_ref, wglr_ref, wg2_ref, bg_ref)
    proj_ref[...] = proj.reshape(g_, r_, QKVR)
    la_ref[...] = log_a.reshape(g_, r_, DK_ALL)


def _gla_scan_sample_kernel(proj_ref, la_ref, s0_ref, o_ref, sout_ref, xt_scr):
    qscale = GLA_DK ** -0.5
    n_kind = DEC_SEQ * GLA_HEADS
    xt_scr[...] = jnp.zeros_like(xt_scr)

    def seq(j, carry):
        for t in range(DEC_SEQ):
            a_t = jnp.exp(la_ref[t, pl.ds(j, 1), :])
            q_t = proj_ref[t, pl.ds(j, 1), 0:DK_ALL] * qscale
            k_t = proj_ref[t, pl.ds(j, 1), DK_ALL:2 * DK_ALL]
            for h in range(GLA_HEADS):
                ks = slice(h * GLA_DK, (h + 1) * GLA_DK)
                row = h * DEC_SEQ + t
                xt_scr[row:row + 1, :] = a_t[:, ks]
                xt_scr[n_kind + row:n_kind + row + 1, :] = q_t[:, ks]
                xt_scr[2 * n_kind + row:2 * n_kind + row + 1, :] = k_t[:, ks]
        xt = xt_scr[...].T
        for h in range(GLA_HEADS):
            s = s0_ref[j, h]
            for t in range(DEC_SEQ):
                row = h * DEC_SEQ + t
                a_c = xt[:, row:row + 1]
                q_c = xt[:, n_kind + row:n_kind + row + 1]
                k_c = xt[:, 2 * n_kind + row:2 * n_kind + row + 1]
                vlo = 2 * DK_ALL + h * GLA_DV
                v_t = proj_ref[t, pl.ds(j, 1), vlo:vlo + GLA_DV]
                s = a_c * s + k_c * v_t
                o_ref[t, pl.ds(j, 1), h * GLA_DV:(h + 1) * GLA_DV] = jnp.sum(q_c * s, axis=0, keepdims=True)
            sout_ref[j, h] = s
        return carry

    lax.fori_loop(0, SCAN_BS, seq, 0)


def _gla_out_sample_kernel(x_ref, gt_ref, oscan_ref, r_ref, gout_ref, wout_ref, o_ref):
    x = x_ref[...]
    g_, r_, _ = x.shape
    n = g_ * r_
    mix = _gla_out(oscan_ref[...].reshape(n, DV_ALL), r_ref[...].reshape(n, DV_ALL), gout_ref[...], wout_ref)
    o_ref[...] = x + _get_mod(gt_ref, False) * mix.reshape(g_, r_, D_MODEL)


def _gla_sample(x, mod_all, g, wqkvr, wglr, wg2, bg, gout, wout, s0):
    tok = _Tok(False, 0)
    proj, log_a = pl.pallas_call(
        _gla_in_sample_kernel,
        grid=tok.grid,
        in_specs=[tok.x(D_MODEL), tok.mod(_mod_col(0, 1, 0)), tok.mod(_mod_col(0, 1, 1)),
                  _resident((1, D_MODEL)), _resident(wqkvr.shape), _resident(wglr.shape), _resident(wg2.shape),
                  _resident(bg.shape)],
        out_specs=[tok.x(QKVR), tok.x(DK_ALL)],
        out_shape=[jax.ShapeDtypeStruct(tok.shape(QKVR), F32), jax.ShapeDtypeStruct(tok.shape(DK_ALL), F32)],
        compiler_params=_params(),
        name="gla_in_sample",
    )(x, mod_all, mod_all, g, wqkvr, wglr, wg2, bg)

    state_spec = pl.BlockSpec((SCAN_BS, GLA_HEADS, GLA_DK, GLA_DV), lambda j: (j, 0, 0, 0))
    oscan, s_out = pl.pallas_call(
        _gla_scan_sample_kernel,
        grid=(DEC_BATCH // SCAN_BS,),
        in_specs=[pl.BlockSpec((DEC_SEQ, SCAN_BS, QKVR), lambda j: (0, j, 0)),
                  pl.BlockSpec((DEC_SEQ, SCAN_BS, DK_ALL), lambda j: (0, j, 0)),
                  state_spec],
        out_specs=[pl.BlockSpec((DEC_SEQ, SCAN_BS, DV_ALL), lambda j: (0, j, 0)), state_spec],
        out_shape=[jax.ShapeDtypeStruct(tok.shape(DV_ALL), F32), jax.ShapeDtypeStruct(s0.shape, F32)],
        scratch_shapes=[pltpu.VMEM((LANE, LANE), F32)],
        compiler_params=pltpu.CompilerParams(dimension_semantics=("arbitrary",), vmem_limit_bytes=VMEM_LIMIT),
        name="gla_scan_sample",
    )(proj, log_a, s0)

    h = pl.pallas_call(
        _gla_out_sample_kernel,
        grid=tok.grid,
        in_specs=[tok.x(D_MODEL), tok.mod(_mod_col(0, 1, 2)), tok.x(DV_ALL),
                  tok.x(DV_ALL, col=(2 * DK_ALL + DV_ALL) // DV_ALL), _resident(gout.shape), _resident(wout.shape)],
        out_specs=tok.x(D_MODEL),
        out_shape=jax.ShapeDtypeStruct(x.shape, F32),
        compiler_params=_params(),
        name="gla_out_sample",
    )(x, mod_all, oscan, proj, gout, wout)
    return h, s_out


def _kv_kernel(x_ref, sh_ref, sc_ref, g_ref, wkv_ref, wf_ref, bf_ref, gk_ref, *refs, prompt):
    if prompt:
        k_ref, v_ref, lf_ref, kb_ref, vb_ref, f_ref, ft_ref, carry_scr = refs
    else:
        k_ref, v_ref, lf_ref = refs
    x = x_ref[...]
    g_, r_, _ = x.shape
    n = g_ * r_
    u = _modulate(x, g_ref[...], _get_mod(sh_ref, prompt), _get_mod(sc_ref, prompt))
    u = u.reshape(n, D_MODEL).astype(BF16)
    kv = _dot(u, wkv_ref[...])
    k = kv[:, :D_MODEL]
    v = kv[:, D_MODEL:]
    kn = k * _head_inv_rms(k) * gk_ref[...]
    lf = _log_sigmoid(_dot(u, wf_ref[...]) + bf_ref[...])
    k_ref[...] = kn.reshape(g_, r_, D_MODEL)
    v_ref[...] = v.reshape(g_, r_, D_MODEL)
    lf_ref[...] = lf[:, :FOX_HEADS].reshape(g_, r_, FOX_HEADS)
    if prompt:
        kb_ref[...] = kn.astype(BF16).reshape(g_, r_, D_MODEL)
        vb_ref[...] = v.astype(BF16).reshape(g_, r_, D_MODEL)

        @pl.when(pl.program_id(1) == 0)
        def _():
            carry_scr[...] = jnp.zeros_like(carry_scr)

        tri = jnp.where(_lower_tri(n), 1.0, 0.0).astype(BF16)
        fsum = _split_dot(tri, lf, 3) + carry_scr[...]
        carry_scr[...] = fsum[n - 1:n, :]
        f_ref[...] = fsum[:, :FOX_HEADS].reshape(g_, r_, FOX_HEADS)
        ft_ref[0] = fsum.T[:FOX_HEADS, :]


def _kv(tok, x, mod_all, g, wkv, wf, bf, gk, name):
    prompt = tok.prompt
    out_specs = [tok.x(D_MODEL), tok.x(D_MODEL), tok.x(FOX_HEADS)]
    out_shape = [jax.ShapeDtypeStruct(tok.shape(D_MODEL), F32), jax.ShapeDtypeStruct(tok.shape(D_MODEL), F32),
                 jax.ShapeDtypeStruct(tok.shape(FOX_HEADS), F32)]
    scratch = []
    if prompt:
        out_specs += [tok.x(D_MODEL), tok.x(D_MODEL), tok.x(FOX_HEADS),
                      pl.BlockSpec((1, FOX_HEADS, TM_KV), lambda b, i: (b, 0, i))]
        out_shape += [jax.ShapeDtypeStruct(tok.shape(D_MODEL), BF16), jax.ShapeDtypeStruct(tok.shape(D_MODEL), BF16),
                      jax.ShapeDtypeStruct(tok.shape(FOX_HEADS), F32),
                      jax.ShapeDtypeStruct((BATCH, FOX_HEADS, SEQ), F32)]
        scratch = [pltpu.VMEM((1, LANE), F32)]
    return pl.pallas_call(
        functools.partial(_kv_kernel, prompt=prompt),
        grid=tok.grid,
        in_specs=[tok.x(D_MODEL), tok.mod(18), tok.mod(19), _resident((1, D_MODEL)), _resident(wkv.shape),
                  _resident(wf.shape), _resident(bf.shape), _resident(gk.shape)],
        out_specs=out_specs,
        out_shape=out_shape,
        scratch_shapes=scratch,
        compiler_params=_params(),
        name=name,
    )(x, mod_all, mod_all, g, wkv, wf, bf, gk)


def _fox_q(u, wqg_ref, gq_ref):
    qg = _dot(u, wqg_ref[...])
    q = qg[:, :D_MODEL]
    og = qg[:, D_MODEL:]
    qn = q * _head_inv_rms(q) * gq_ref[...] * (FOX_HD ** -0.5 * LOG2E)
    return qn, og


def _softmax_step(t, row_bias, m, l, acc, v):
    m_cur = jnp.max(t, axis=-1, keepdims=True)
    if row_bias is not None:
        m_cur = m_cur + row_bias
    m_new = jnp.maximum(m, m_cur)
    alpha = jnp.exp2(m - m_new)
    shift = -m_new if row_bias is None else row_bias - m_new
    p = jnp.exp2(t + jnp.tile(shift, (1, t.shape[1] // LANE)))
    l_new = alpha * l + jnp.sum(p, axis=-1, keepdims=True)
    pv = _dot(p.astype(BF16), v)
    acc_new = jnp.tile(alpha, (1, pv.shape[1] // LANE)) * acc + pv
    return m_new, l_new, acc_new


def _fox_prompt_kernel(x_ref, sh_ref, sc_ref, gt_ref, g_ref, wqg_ref, gq_ref, wo_ref, kb_ref, vb_ref, f_ref, ft_ref,
                       o_ref, qm_scr, og_scr, oatt_scr, m_scr, l_scr, acc_scr, fq_scr):
    i = pl.program_id(1)
    tq = x_ref.shape[1]
    x = x_ref[0]
    u = _modulate(x, g_ref[...], _get_mod(sh_ref, True), _get_mod(sc_ref, True)).astype(BF16)
    qn, og = _fox_q(u, wqg_ref, gq_ref)
    og_scr[...] = og
    first = lax.broadcasted_iota(jnp.int32, (tq, LANE), 1) < FOX_HD

    for pair in range(FOX_HEADS // 2):
        q2 = qn[:, pair * LANE:(pair + 1) * LANE]
        qm_scr[2 * pair] = jnp.where(first, q2, 0.0).astype(BF16)
        qm_scr[2 * pair + 1] = jnp.where(first, 0.0, q2).astype(BF16)

    fq_all = f_ref[0]
    causal = _lower_tri(tq)

    for grp in range(FOX_HEADS // FOX_GROUP):
        m_scr[...] = jnp.full_like(m_scr, NEG)
        l_scr[...] = jnp.zeros_like(l_scr)
        acc_scr[...] = jnp.zeros_like(acc_scr)
        for idx in range(FOX_GROUP):
            head = grp * FOX_GROUP + idx
            fq_scr[idx] = jnp.broadcast_to(fq_all[:, head:head + 1], (tq, LANE)) * LOG2E

        def step(j, diagonal, grp=grp):
            rows = pl.ds(pl.multiple_of(j * tq, tq), tq)
            for idx in range(FOX_GROUP):
                head = grp * FOX_GROUP + idx
                cols = slice((head // 2) * LANE, (head // 2 + 1) * LANE)
                fk = ft_ref[0, head, pl.ds(j, 1), :] * LOG2E
                t = _dot_nt(qm_scr[head], kb_ref[0, rows, cols]) - fk
                if diagonal:
                    t = jnp.where(causal, t, NEG)
                m, l, acc = _softmax_step(t, fq_scr[idx], m_scr[idx], l_scr[idx], acc_scr[idx],
                                          vb_ref[0, rows, cols])
                m_scr[idx] = m
                l_scr[idx] = l
                acc_scr[idx] = acc

        def off_diagonal(j, carry):
            step(j, False)
            return carry

        lax.fori_loop(0, i, off_diagonal, 0)
        step(i, True)
        for pp in range(FOX_GROUP // 2):
            pair = grp * (FOX_GROUP // 2) + pp
            o2 = jnp.where(first, acc_scr[2 * pp] / l_scr[2 * pp], acc_scr[2 * pp + 1] / l_scr[2 * pp + 1])
            oatt_scr[:, pair * LANE:(pair + 1) * LANE] = o2

    gated = (oatt_scr[...] * jax.nn.sigmoid(og_scr[...])).astype(BF16)
    o_ref[0] = x + _get_mod(gt_ref, True) * _dot(gated, wo_ref[...])


def _fox_prompt(x, mod_all, g, wqg, gq, wo, kb, vb, fsum, fsum_t):
    tok = _Tok(True, TQ_FOX)
    nk = SEQ // TQ_FOX
    seq_spec = pl.BlockSpec((1, SEQ, D_MODEL), lambda b, i: (b, 0, 0), pipeline_mode=pl.Buffered(1))
    return pl.pallas_call(
        _fox_prompt_kernel,
        grid=tok.grid,
        in_specs=[tok.x(D_MODEL), tok.mod(_mod_col(1, 1, 0)), tok.mod(_mod_col(1, 1, 1)), tok.mod(_mod_col(1, 1, 2)),
                  _resident((1, D_MODEL)), _resident(wqg.shape), _resident(gq.shape), _resident(wo.shape),
                  seq_spec, seq_spec, tok.x(FOX_HEADS),
                  pl.BlockSpec((1, FOX_HEADS, nk, TQ_FOX), lambda b, i: (b, 0, 0, 0))],
        out_specs=tok.x(D_MODEL),
        out_shape=jax.ShapeDtypeStruct(x.shape, F32),
        scratch_shapes=[pltpu.VMEM((FOX_HEADS, TQ_FOX, LANE), BF16), pltpu.VMEM((TQ_FOX, D_MODEL), F32),
                        pltpu.VMEM((TQ_FOX, D_MODEL), F32), pltpu.VMEM((FOX_GROUP, TQ_FOX, LANE), F32),
                        pltpu.VMEM((FOX_GROUP, TQ_FOX, LANE), F32), pltpu.VMEM((FOX_GROUP, TQ_FOX, LANE), F32),
                        pltpu.VMEM((FOX_GROUP, TQ_FOX, LANE), F32)],
        compiler_params=_params(),
        name="fox_prompt",
    )(x, mod_all, mod_all, mod_all, g, wqg, gq, wo, kb, vb, fsum, fsum_t.reshape(BATCH, FOX_HEADS, nk, TQ_FOX))


def _fox_q_sample_kernel(x_ref, sh_ref, sc_ref, g_ref, wqg_ref, gq_ref, q_ref, og_ref):
    x = x_ref[...]
    g_, r_, _ = x.shape
    u = _modulate(x, g_ref[...], _get_mod(sh_ref, False), _get_mod(sc_ref, False))
    qn, og = _fox_q(u.reshape(g_ * r_, D_MODEL).astype(BF16), wqg_ref, gq_ref)
    q_ref[...] = qn.reshape(g_, r_, D_MODEL)
    og_ref[...] = og.reshape(g_, r_, D_MODEL)


def _fox_out_sample_kernel(x_ref, gt_ref, oatt_ref, og_ref, wo_ref, o_ref):
    x = x_ref[...]
    g_, r_, _ = x.shape
    n = g_ * r_
    gated = (oatt_ref[...] * jax.nn.sigmoid(og_ref[...])).reshape(n, D_MODEL).astype(BF16)
    o_ref[...] = x + _get_mod(gt_ref, False) * _dot(gated, wo_ref[...]).reshape(g_, r_, D_MODEL)


def _paged_kernel(pt_ref, q_ref, kn_ref, vn_ref, lnt_ref, *refs):
    del pt_ref
    k_pages = refs[:N_PAGES]
    v_pages = refs[N_PAGES:2 * N_PAGES]
    l_pages = refs[2 * N_PAGES:3 * N_PAGES]
    o_ref = refs[3 * N_PAGES]
    kn_scr, vn_scr = refs[3 * N_PAGES + 1:]
    nrow = DEC_SEQ * FOX_HEADS

    hrow = lax.broadcasted_iota(jnp.int32, (FOX_HEADS, D_MODEL), 0)
    hcol = lax.broadcasted_iota(jnp.int32, (FOX_HEADS, D_MODEL), 1)
    head_mask = (hcol >> HD_SHIFT) == hrow
    q = q_ref[0]
    qbd = jnp.concatenate(
        [jnp.where(head_mask, jnp.broadcast_to(q[t:t + 1, :], (FOX_HEADS, D_MODEL)), 0.0) for t in range(DEC_SEQ)],
        axis=0).astype(BF16)

    lnt = lnt_ref[0]
    cn = [lnt[:, 0:1]]
    for t in range(1, DEC_SEQ):
        cn.append(cn[-1] + lnt[:, t:t + 1])
    cn_col = jnp.concatenate(cn, axis=0)

    ri = lax.broadcasted_iota(jnp.int32, (PAGE_SIZE, 2 * LANE), 0)
    ci = lax.broadcasted_iota(jnp.int32, (PAGE_SIZE, 2 * LANE), 1)
    suffix = jnp.where(((ci < PAGE_SIZE) & (ri > ci)) | (ci == PAGE_SIZE), 1.0, 0.0).astype(BF16)

    logits = [None] * N_PAGES
    tot = jnp.zeros((FOX_HEADS, 1), F32)
    for idx in reversed(range(N_PAGES)):
        sums = _split_dot_lhs(l_pages[idx][0], suffix, 2)
        bias16 = sums[:, :PAGE_SIZE] + tot
        tot = tot + sums[:, PAGE_SIZE:PAGE_SIZE + 1]
        bias = jnp.concatenate([bias16] * DEC_SEQ, axis=0) + cn_col
        logits[idx] = _dot_nt(qbd, k_pages[idx][0]) + bias * LOG2E

    kn_scr[...] = jnp.zeros_like(kn_scr)
    vn_scr[...] = jnp.zeros_like(vn_scr)
    kn_scr[0:DEC_SEQ, :] = kn_ref[0]
    vn_scr[0:DEC_SEQ, :] = vn_ref[0]
    lane = lax.broadcasted_iota(jnp.int32, (nrow, LANE), 1)
    step_of_row = lax.broadcasted_iota(jnp.int32, (nrow, LANE), 0) >> HEAD_SHIFT
    cn_keys = jnp.zeros((nrow, LANE), F32)
    for t in range(DEC_SEQ):
        cn_keys = jnp.where(lane == t, jnp.concatenate([cn[t]] * DEC_SEQ, axis=0), cn_keys)
    s_new = _dot_nt(qbd, kn_scr[...].astype(BF16)) + (cn_col - cn_keys) * LOG2E
    logits.append(jnp.where(lane <= step_of_row, s_new, NEG))
    values = [v_pages[idx][0] for idx in range(N_PAGES)] + [vn_scr[...].astype(BF16)]

    m = logits[0]
    for s in logits[1:]:
        m = jnp.maximum(m, s)
    m = jnp.broadcast_to(jnp.max(m, axis=-1, keepdims=True), (nrow, LANE))
    l = jnp.zeros((nrow, LANE), F32)
    acc = jnp.zeros((nrow, D_MODEL), F32)
    for s, v in zip(logits, values):
        p = jnp.exp2(s - m)
        l = l + p
        acc = acc + _dot(p.astype(BF16), v)
    l = jnp.broadcast_to(jnp.sum(l, axis=-1, keepdims=True), (nrow, LANE))
    out = acc / jnp.tile(l, (1, D_MODEL // LANE))
    for t in range(DEC_SEQ):
        blk = out[t * FOX_HEADS:(t + 1) * FOX_HEADS, :]
        o_ref[0, t:t + 1, :] = jnp.sum(jnp.where(head_mask, blk, 0.0), axis=0, keepdims=True)


def _paged_attend(q, k_new, v_new, lf_new_t, cache_k, cache_v, cache_lf_t, page_table):
    def page_spec(shape, idx):
        return pl.BlockSpec((1,) + shape, lambda b, pt: (pt[b, idx], 0, 0))

    seq_spec = pl.BlockSpec((1, DEC_SEQ, D_MODEL), lambda b, pt: (b, 0, 0))
    in_specs = [seq_spec, seq_spec, seq_spec, pl.BlockSpec((1, FOX_HEADS, DEC_SEQ), lambda b, pt: (b, 0, 0))]
    in_specs += [page_spec((PAGE_SIZE, D_MODEL), i) for i in range(N_PAGES)]
    in_specs += [page_spec((PAGE_SIZE, D_MODEL), i) for i in range(N_PAGES)]
    in_specs += [page_spec((FOX_HEADS, PAGE_SIZE), i) for i in range(N_PAGES)]
    grid_spec = pltpu.PrefetchScalarGridSpec(
        num_scalar_prefetch=1,
        grid=(DEC_BATCH,),
        in_specs=in_specs,
        out_specs=seq_spec,
        scratch_shapes=[pltpu.VMEM((PAGE_SIZE, D_MODEL), F32), pltpu.VMEM((PAGE_SIZE, D_MODEL), F32)],
    )
    return pl.pallas_call(
        _paged_kernel,
        grid_spec=grid_spec,
        out_shape=jax.ShapeDtypeStruct((DEC_BATCH, DEC_SEQ, D_MODEL), F32),
        compiler_params=pltpu.CompilerParams(dimension_semantics=("arbitrary",), vmem_limit_bytes=VMEM_LIMIT),
        name="fox_paged",
    )(page_table, q, k_new, v_new, lf_new_t, *([cache_k] * N_PAGES), *([cache_v] * N_PAGES),
      *([cache_lf_t] * N_PAGES))


def _fox_sample(x, mod_all, g, wqg, gq, wo, k_new, v_new, lf_new, cache_k, cache_v, cache_logf, page_table):
    tok = _Tok(False, 0)
    qn, og = pl.pallas_call(
        _fox_q_sample_kernel,
        grid=tok.grid,
        in_specs=[tok.x(D_MODEL), tok.mod(_mod_col(1, 1, 0)), tok.mod(_mod_col(1, 1, 1)), _resident((1, D_MODEL)),
                  _resident(wqg.shape), _resident(gq.shape)],
        out_specs=[tok.x(D_MODEL), tok.x(D_MODEL)],
        out_shape=[jax.ShapeDtypeStruct(x.shape, F32), jax.ShapeDtypeStruct(x.shape, F32)],
        compiler_params=_params(),
        name="fox_q_sample",
    )(x, mod_all, mod_all, g, wqg, gq)

    n_phys = cache_k.shape[0]
    oatt = _paged_attend(
        qn.transpose(1, 0, 2), k_new.transpose(1, 0, 2), v_new.transpose(1, 0, 2), lf_new.transpose(1, 2, 0),
        cache_k.reshape(n_phys, PAGE_SIZE, D_MODEL).astype(BF16),
        cache_v.reshape(n_phys, PAGE_SIZE, D_MODEL).astype(BF16),
        cache_logf.transpose(0, 2, 1), page_table)

    return pl.pallas_call(
        _fox_out_sample_kernel,
        grid=tok.grid,
        in_specs=[tok.x(D_MODEL), tok.mod(_mod_col(1, 1, 2)), tok.x(D_MODEL), tok.x(D_MODEL), _resident(wo.shape)],
        out_specs=tok.x(D_MODEL),
        out_shape=jax.ShapeDtypeStruct(x.shape, F32),
        compiler_params=_params(),
        name="fox_out_sample",
    )(x, mod_all, oatt.transpose(1, 0, 2), og, wo)


def kernel(x_prompt, x_sample, state_gla, cache_k, cache_v, cache_logf, page_table, c_prompt, c_sample, w_ada, b_ada, g_norm, w_ffn_up, w_ffn_down, gla_w_in, gla_w_gate2, gla_b_gate, gla_g_out, gla_w_out, w_ada_kv, b_ada_kv, g_kv, w_kvf, b_f, g_k, fox_w_qg, fox_g_q, fox_w_o):
    w_mod = jnp.concatenate([w_ada[0], w_ada[1], w_ada_kv], axis=1).astype(BF16)
    b_mod = jnp.concatenate([b_ada[0], b_ada[1], b_ada_kv], axis=0)[None, :]
    c_all = jnp.concatenate([c_sample, c_prompt, jnp.zeros((MOD_ROWS - DEC_BATCH - BATCH, D_MODEL), F32)], axis=0)
    wu = w_ffn_up.astype(BF16)
    wd = w_ffn_down.astype(BF16)
    w_in = gla_w_in[0]
    wqkvr = w_in[:, :QKVR].astype(BF16)
    wglr = jnp.pad(w_in[:, QKVR:], ((0, 0), (0, LANE - GLA_RANK))).astype(BF16)
    wg2 = jnp.pad(gla_w_gate2[0], ((0, LANE - GLA_RANK), (0, 0))).astype(BF16)
    bg = gla_b_gate[0][None, :]
    gout = gla_g_out[0][None, :]
    wout = gla_w_out[0].astype(BF16)
    wkv = w_kvf[:, :2 * D_MODEL].astype(BF16)
    wf = jnp.pad(w_kvf[:, 2 * D_MODEL:], ((0, 0), (0, LANE - FOX_HEADS))).astype(BF16)
    bf = jnp.pad(b_f, (0, LANE - FOX_HEADS))[None, :]
    gk = jnp.tile(g_k, FOX_HEADS)[None, :]
    wqg = fox_w_qg[0].astype(BF16)
    gq = jnp.tile(fox_g_q[0], FOX_HEADS)[None, :]
    wo = fox_w_o[0].astype(BF16)
    gkv = g_kv[None, :]

    def gn(layer, sub):
        return g_norm[layer, sub][None, :]

    mod_all = _ada(c_all, w_mod, b_mod)

    tok = _Tok(True, TM_FFN)
    h = _ffn(tok, x_prompt, mod_all, 0, 0, gn(0, 0), wu, wd,"ffn_p00")
    s0 = jnp.zeros((BATCH, GLA_HEADS, GLA_DK, GLA_DV), F32)
    h, sg_prompt = _gla_prompt(h, mod_all, gn(0, 1), wqkvr, wglr, wg2, bg, gout, wout, s0)
    h = _ffn(tok, h, mod_all, 0, 2, gn(0, 2), wu, wd,"ffn_p02")
    k_p, v_p, lf_p, kb, vb, fsum, fsum_t = _kv(_Tok(True, TM_KV), h, mod_all, gkv, wkv, wf, bf, gk, "kv_prompt")
    h = _ffn(tok, h, mod_all, 1, 0, gn(1, 0), wu, wd,"ffn_p10")
    h = _fox_prompt(h, mod_all, gn(1, 1), wqg, gq, wo, kb, vb, fsum, fsum_t)
    y_prompt = _ffn(tok, h, mod_all, 1, 2, gn(1, 2), wu, wd,"ffn_p12")

    tok = _Tok(False, 0)
    hs = x_sample.transpose(1, 0, 2)
    hs = _ffn(tok, hs, mod_all, 0, 0, gn(0, 0), wu, wd,"ffn_s00")
    hs, sg_sample = _gla_sample(hs, mod_all, gn(0, 1), wqkvr, wglr, wg2, bg, gout, wout, state_gla[0])
    hs = _ffn(tok, hs, mod_all, 0, 2, gn(0, 2), wu, wd,"ffn_s02")
    k_s, v_s, lf_s = _kv(tok, hs, mod_all, gkv, wkv, wf, bf, gk, "kv_sample")
    hs = _ffn(tok, hs, mod_all, 1, 0, gn(1, 0), wu, wd,"ffn_s10")
    hs = _fox_sample(hs, mod_all, gn(1, 1), wqg, gq, wo, k_s, v_s, lf_s, cache_k, cache_v, cache_logf, page_table)
    hs = _ffn(tok, hs, mod_all, 1, 2, gn(1, 2), wu, wd,"ffn_s12")
    y_sample = hs.transpose(1, 0, 2)

    def heads(t, lead):
        return t.reshape(lead + (FOX_HEADS, FOX_HD))

    return (y_prompt, y_sample, sg_prompt[None],
            heads(k_p, (BATCH, SEQ)), heads(v_p, (BATCH, SEQ)), lf_p,
            sg_sample[None],
            heads(k_s.transpose(1, 0, 2), (DEC_BATCH, DEC_SEQ)), heads(v_s.transpose(1, 0, 2), (DEC_BATCH, DEC_SEQ)),
            lf_s.transpose(1, 0, 2))
```

```python
import functools

import jax
import jax.numpy as jnp
from jax import lax
from jax.experimental import pallas as pl
from jax.experimental.pallas import tpu as pltpu

F32 = jnp.float32
BF16 = jnp.bfloat16

D_MODEL = 1024
BATCH = 4
SEQ = 4096
DEC_BATCH = 128
DEC_SEQ = 4
PAST_LEN = 2048
PAGE_SIZE = 128
N_PAGES = PAST_LEN // PAGE_SIZE
GLA_HEADS = 4
GLA_DK = 128
GLA_DV = 256
GLA_RANK = 16
GLA_TAU = 16.0
FOX_HEADS = 16
FOX_HD = 64
D_FF = 2816
NORM_EPS = 1e-6
HD_SHIFT = FOX_HD.bit_length() - 1
HEAD_SHIFT = FOX_HEADS.bit_length() - 1

DK_ALL = GLA_HEADS * GLA_DK
DV_ALL = GLA_HEADS * GLA_DV
QKVR = 2 * DK_ALL + 2 * DV_ALL
LANE = 128
SUBLANE = 8

NEG = -1e30
LOG2E = 1.4426950408889634
VMEM_LIMIT = 56 * 1024 * 1024

TM_FFN = 512
FF_CHUNK = 1408
TM_GLA = 512
GLA_CHUNK = 128
TM_KV = 512
TQ_FOX = 512
FOX_GROUP = 4
ADA_TN = 2048
SCAN_BS = 8

N_MOD_BLOCKS = 2 * 9 + 2
MOD_ROWS = DEC_BATCH + SUBLANE
PROMPT_ROW_BLOCK = DEC_BATCH // SUBLANE


def _mod_col(layer, sub, kind):
    return layer * 9 + sub * 3 + kind


def _dot(a, b):
    return jnp.dot(a, b, preferred_element_type=F32)


def _dot_nt(a, b):
    return lax.dot_general(a, b, (((1,), (1,)), ((), ())), preferred_element_type=F32)


def _dot_tn(a, b):
    return lax.dot_general(a, b, (((0,), (0,)), ((), ())), preferred_element_type=F32)


def _split_dot(a_bf, x, terms):
    out = None
    r = x
    for _ in range(terms):
        p = r.astype(BF16)
        r = r - p.astype(F32)
        d = _dot(a_bf, p)
        out = d if out is None else out + d
    return out


def _split_dot_lhs(x, b_bf, terms):
    out = None
    r = x
    for _ in range(terms):
        p = r.astype(BF16)
        r = r - p.astype(F32)
        d = _dot(p, b_bf)
        out = d if out is None else out + d
    return out


def _log_sigmoid(x):
    return jnp.minimum(x, 0.0) - jnp.log1p(jnp.exp(-jnp.abs(x)))


def _rms(x):
    return x * lax.rsqrt(jnp.mean(x * x, axis=-1, keepdims=True) + NORM_EPS)


def _modulate(x, g, shift, scale):
    return (_rms(x) * g) * (1.0 + scale) + shift


def _get_mod(ref, prompt):
    if prompt:
        return ref[pl.ds(pl.program_id(0), 1), :]
    return ref[...]


def _lower_tri(n, strict=False):
    r = lax.broadcasted_iota(jnp.int32, (n, n), 0)
    c = lax.broadcasted_iota(jnp.int32, (n, n), 1)
    return (r > c) if strict else (r >= c)


def _head_indicator(transposed):
    if transposed:
        h = lax.broadcasted_iota(jnp.int32, (LANE, D_MODEL), 0)
        c = lax.broadcasted_iota(jnp.int32, (LANE, D_MODEL), 1)
    else:
        c = lax.broadcasted_iota(jnp.int32, (D_MODEL, LANE), 0)
        h = lax.broadcasted_iota(jnp.int32, (D_MODEL, LANE), 1)
    return jnp.where((c >> HD_SHIFT) == h, 1.0, 0.0).astype(BF16)


def _head_inv_rms(x):
    ss = _split_dot_lhs(x * x, _head_indicator(False), 2)
    inv = lax.rsqrt(ss * (1.0 / FOX_HD) + NORM_EPS)
    return _split_dot_lhs(inv, _head_indicator(True), 2)


class _Tok:
    def __init__(self, prompt, tm):
        self.prompt = prompt
        if prompt:
            self.grid = (BATCH, SEQ // tm)
            self.rows = (1, tm)
            self.mod_block = (SUBLANE, D_MODEL)
            self.mod_row = PROMPT_ROW_BLOCK
        else:
            self.grid = (1, 1)
            self.rows = (DEC_SEQ, DEC_BATCH)
            self.mod_block = (DEC_BATCH, D_MODEL)
            self.mod_row = 0

    def x(self, n, col=0):
        if self.prompt:
            return pl.BlockSpec(self.rows + (n,), lambda b, i: (b, i, col))
        return pl.BlockSpec(self.rows + (n,), lambda b, i: (0, 0, col))

    def mod(self, col):
        row = self.mod_row
        return pl.BlockSpec(self.mod_block, lambda b, i: (row, col))

    def shape(self, n):
        return (BATCH, SEQ, n) if self.prompt else (DEC_SEQ, DEC_BATCH, n)


def _resident(shape):
    nd = len(shape)
    return pl.BlockSpec(shape, lambda *_: (0,) * nd, pipeline_mode=pl.Buffered(1))


def _resident_at(shape, lead):
    nd = len(shape) - len(lead)
    block = (None,) * len(lead) + tuple(shape[len(lead):])
    return pl.BlockSpec(block, lambda *_: tuple(lead) + (0,) * nd, pipeline_mode=pl.Buffered(1))


def _params():
    return pltpu.CompilerParams(dimension_semantics=("arbitrary", "arbitrary"), vmem_limit_bytes=VMEM_LIMIT)


def _ada_kernel(c_ref, w_ref, b_ref, o_ref):
    sc = jax.nn.silu(c_ref[...]).astype(BF16)
    o_ref[...] = _dot(sc, w_ref[...]) + b_ref[...]


def _ada(c_all, w_all, b_all):
    n = w_all.shape[1]
    return pl.pallas_call(
        _ada_kernel,
        grid=(n // ADA_TN,),
        in_specs=[pl.BlockSpec((MOD_ROWS, D_MODEL), lambda j: (0, 0)),
                  pl.BlockSpec((D_MODEL, ADA_TN), lambda j: (0, j)),
                  pl.BlockSpec((1, ADA_TN), lambda j: (0, j))],
        out_specs=pl.BlockSpec((MOD_ROWS, ADA_TN), lambda j: (0, j)),
        out_shape=jax.ShapeDtypeStruct((MOD_ROWS, n), F32),
        compiler_params=pltpu.CompilerParams(dimension_semantics=("arbitrary",), vmem_limit_bytes=VMEM_LIMIT),
        name="ada_mod",
    )(c_all, w_all, b_all)


def _ffn_kernel(x_ref, sh_ref, sc_ref, gt_ref, g_ref, wu_ref, wd_ref, o_ref, *, prompt):
    x = x_ref[...]
    g_, r_, _ = x.shape
    n = g_ * r_
    u = _modulate(x, g_ref[...], _get_mod(sh_ref, prompt), _get_mod(sc_ref, prompt))
    u = u.reshape(n, D_MODEL).astype(BF16)
    acc = None
    for c in range(D_FF // FF_CHUNK):
        lo = c * FF_CHUNK
        a = _dot(u, wu_ref[:, lo:lo + FF_CHUNK])
        b = _dot(u, wu_ref[:, D_FF + lo:D_FF + lo + FF_CHUNK])
        gated = (jax.nn.silu(a) * b).astype(BF16)
        part = _dot(gated, wd_ref[lo:lo + FF_CHUNK, :])
        acc = part if acc is None else acc + part
    o_ref[...] = x + (0.5 * _get_mod(gt_ref, prompt)) * acc.reshape(g_, r_, D_MODEL)


def _ffn(tok, x, mod_all, layer, sub, g, wu, wd, name):
    which = (layer, sub // 2)
    return pl.pallas_call(
        functools.partial(_ffn_kernel, prompt=tok.prompt),
        grid=tok.grid,
        in_specs=[tok.x(D_MODEL),
                  tok.mod(_mod_col(layer, sub, 0)), tok.mod(_mod_col(layer, sub, 1)), tok.mod(_mod_col(layer, sub, 2)),
                  _resident((1, D_MODEL)), _resident_at(wu.shape, which), _resident_at(wd.shape, which)],
        out_specs=tok.x(D_MODEL),
        out_shape=jax.ShapeDtypeStruct(x.shape, F32),
        compiler_params=_params(),
        name=name,
    )(x, mod_all, mod_all, mod_all, g, wu, wd)


def _gla_in(u, wqkvr_ref, wglr_ref, wg2_ref, bg_ref):
    proj = _dot(u, wqkvr_ref[...])
    glr = _dot(u, wglr_ref[...])
    xg = _dot(glr.astype(BF16), wg2_ref[...]) + bg_ref[...]
    return proj, _log_sigmoid(xg) * (1.0 / GLA_TAU)


def _gla_out(o, r, gout, wout_ref):
    heads = []
    for h in range(GLA_HEADS):
        oh = o[:, h * GLA_DV:(h + 1) * GLA_DV]
        heads.append(_rms(oh) * gout)
    y = (jnp.concatenate(heads, axis=-1) * jax.nn.silu(r)).astype(BF16)
    return _dot(y, wout_ref[...])


def _gla_prompt_kernel(x_ref, sh_ref, sc_ref, gt_ref, g_ref, wqkvr_ref, wglr_ref, wg2_ref, bg_ref, gout_ref,
                       wout_ref, s0_ref, o_ref, sout_ref, proj_scr, la_scr, oscan_scr, st_scr):
    i = pl.program_id(1)
    tm = x_ref.shape[1]
    x = x_ref[0]
    u = _modulate(x, g_ref[...], _get_mod(sh_ref, True), _get_mod(sc_ref, True)).astype(BF16)
    proj, log_a = _gla_in(u, wqkvr_ref, wglr_ref, wg2_ref, bg_ref)
    proj_scr[...] = proj
    la_scr[...] = log_a

    @pl.when(i == 0)
    def _():
        for h in range(GLA_HEADS):
            st_scr[h] = s0_ref[0, h].T

    tri = jnp.where(_lower_tri(GLA_CHUNK), 1.0, 0.0).astype(BF16)
    causal = _lower_tri(GLA_CHUNK)
    qscale = GLA_DK ** -0.5

    def chunk(c, carry):
        rows = pl.ds(pl.multiple_of(c * GLA_CHUNK, GLA_CHUNK), GLA_CHUNK)
        bcum = _split_dot(tri, la_scr[rows, :], 3)
        blast = bcum[GLA_CHUNK - 1:GLA_CHUNK, :]
        q = proj_scr[rows, 0:DK_ALL] * qscale
        k = proj_scr[rows, DK_ALL:2 * DK_ALL]
        qe = (q * jnp.exp(bcum)).astype(BF16)
        kinv = (k * jnp.exp(-bcum)).astype(BF16)
        kd = (k * jnp.exp(blast - bcum)).astype(BF16)
        elast = jnp.exp(blast)
        for h in range(GLA_HEADS):
            ks = slice(h * GLA_DK, (h + 1) * GLA_DK)
            vlo = 2 * DK_ALL + h * GLA_DV
            v = proj_scr[rows, vlo:vlo + GLA_DV].astype(BF16)
            st = st_scr[h]
            att = jnp.where(causal, _dot_nt(qe[:, ks], kinv[:, ks]), 0.0).astype(BF16)
            oscan_scr[rows, h * GLA_DV:(h + 1) * GLA_DV] = _dot_nt(qe[:, ks], st.astype(BF16)) + _dot(att, v)
            st_scr[h] = st * elast[:, ks] + _dot_tn(v, kd[:, ks])
        return carry

    lax.fori_loop(0, tm // GLA_CHUNK, chunk, 0)

    r = proj_scr[:, 2 * DK_ALL + DV_ALL:QKVR]
    mix = _gla_out(oscan_scr[...], r, gout_ref[...], wout_ref)
    o_ref[0] = x + _get_mod(gt_ref, True) * mix

    @pl.when(i == pl.num_programs(1) - 1)
    def _():
        for h in range(GLA_HEADS):
            sout_ref[0, h] = st_scr[h].T


def _gla_prompt(x, mod_all, g, wqkvr, wglr, wg2, bg, gout, wout, s0):
    tok = _Tok(True, TM_GLA)
    state_spec = pl.BlockSpec((1, GLA_HEADS, GLA_DK, GLA_DV), lambda b, i: (b, 0, 0, 0))
    return pl.pallas_call(
        _gla_prompt_kernel,
        grid=tok.grid,
        in_specs=[tok.x(D_MODEL), tok.mod(_mod_col(0, 1, 0)), tok.mod(_mod_col(0, 1, 1)), tok.mod(_mod_col(0, 1, 2)),
                  _resident((1, D_MODEL)), _resident(wqkvr.shape), _resident(wglr.shape), _resident(wg2.shape),
                  _resident(bg.shape), _resident(gout.shape), _resident(wout.shape), state_spec],
        out_specs=[tok.x(D_MODEL), state_spec],
        out_shape=[jax.ShapeDtypeStruct(x.shape, F32), jax.ShapeDtypeStruct(s0.shape, F32)],
        scratch_shapes=[pltpu.VMEM((TM_GLA, QKVR), F32), pltpu.VMEM((TM_GLA, DK_ALL), F32),
                        pltpu.VMEM((TM_GLA, DV_ALL), F32), pltpu.VMEM((GLA_HEADS, GLA_DV, GLA_DK), F32)],
        compiler_params=_params(),
        name="gla_prompt",
    )(x, mod_all, mod_all, mod_all, g, wqkvr, wglr, wg2, bg, gout, wout, s0)


def _gla_in_sample_kernel(x_ref, sh_ref, sc_ref, g_ref, wqkvr_ref, wglr_ref, wg2_ref, bg_ref, proj_ref, la_ref):
    x = x_ref[...]
    g_, r_, _ = x.shape
    u = _modulate(x, g_ref[...], _get_mod(sh_ref, False), _get_mod(sc_ref, False))
    u = u.reshape(g_ * r_, D_MODEL).astype(BF16)
    proj, log_a = _gla_in(u, wqkvr_ref, wglr_ref, wg2_ref, bg_ref)
    proj_ref[...] = proj.reshape(g_, r_, QKVR)
    la_ref[...] = log_a.reshape(g_, r_, DK_ALL)


def _gla_scan_sample_kernel(proj_ref, la_ref, s0_ref, o_ref, sout_ref, xt_scr):
    qscale = GLA_DK ** -0.5
    n_kind = DEC_SEQ * GLA_HEADS
    xt_scr[...] = jnp.zeros_like(xt_scr)

    def seq(j, carry):
        for t in range(DEC_SEQ):
            a_t = jnp.exp(la_ref[t, pl.ds(j, 1), :])
            q_t = proj_ref[t, pl.ds(j, 1), 0:DK_ALL] * qscale
            k_t = proj_ref[t, pl.ds(j, 1), DK_ALL:2 * DK_ALL]
            for h in range(GLA_HEADS):
                ks = slice(h * GLA_DK, (h + 1) * GLA_DK)
                row = h * DEC_SEQ + t
                xt_scr[row:row + 1, :] = a_t[:, ks]
                xt_scr[n_kind + row:n_kind + row + 1, :] = q_t[:, ks]
                xt_scr[2 * n_kind + row:2 * n_kind + row + 1, :] = k_t[:, ks]
        xt = xt_scr[...].T
        for h in range(GLA_HEADS):
            s = s0_ref[j, h]
            for t in range(DEC_SEQ):
                row = h * DEC_SEQ + t
                a_c = xt[:, row:row + 1]
                q_c = xt[:, n_kind + row:n_kind + row + 1]
                k_c = xt[:, 2 * n_kind + row:2 * n_kind + row + 1]
                vlo = 2 * DK_ALL + h * GLA_DV
                v_t = proj_ref[t, pl.ds(j, 1), vlo:vlo + GLA_DV]
                s = a_c * s + k_c * v_t
                o_ref[t, pl.ds(j, 1), h * GLA_DV:(h + 1) * GLA_DV] = jnp.sum(q_c * s, axis=0, keepdims=True)
            sout_ref[j, h] = s
        return carry

    lax.fori_loop(0, SCAN_BS, seq, 0)


def _gla_out_sample_kernel(x_ref, gt_ref, oscan_ref, r_ref, gout_ref, wout_ref, o_ref):
    x = x_ref[...]
    g_, r_, _ = x.shape
    n = g_ * r_
    mix = _gla_out(oscan_ref[...].reshape(n, DV_ALL), r_ref[...].reshape(n, DV_ALL), gout_ref[...], wout_ref)
    o_ref[...] = x + _get_mod(gt_ref, False) * mix.reshape(g_, r_, D_MODEL)


def _gla_sample(x, mod_all, g, wqkvr, wglr, wg2, bg, gout, wout, s0):
    tok = _Tok(False, 0)
    proj, log_a = pl.pallas_call(
        _gla_in_sample_kernel,
        grid=tok.grid,
        in_specs=[tok.x(D_MODEL), tok.mod(_mod_col(0, 1, 0)), tok.mod(_mod_col(0, 1, 1)),
                  _resident((1, D_MODEL)), _resident(wqkvr.shape), _resident(wglr.shape), _resident(wg2.shape),
                  _resident(bg.shape)],
        out_specs=[tok.x(QKVR), tok.x(DK_ALL)],
        out_shape=[jax.ShapeDtypeStruct(tok.shape(QKVR), F32), jax.ShapeDtypeStruct(tok.shape(DK_ALL), F32)],
        compiler_params=_params(),
        name="gla_in_sample",
    )(x, mod_all, mod_all, g, wqkvr, wglr, wg2, bg)

    state_spec = pl.BlockSpec((SCAN_BS, GLA_HEADS, GLA_DK, GLA_DV), lambda j: (j, 0, 0, 0))
    oscan, s_out = pl.pallas_call(
        _gla_scan_sample_kernel,
        grid=(DEC_BATCH // SCAN_BS,),
        in_specs=[pl.BlockSpec((DEC_SEQ, SCAN_BS, QKVR), lambda j: (0, j, 0)),
                  pl.BlockSpec((DEC_SEQ, SCAN_BS, DK_ALL), lambda j: (0, j, 0)),
                  state_spec],
        out_specs=[pl.BlockSpec((DEC_SEQ, SCAN_BS, DV_ALL), lambda j: (0, j, 0)), state_spec],
        out_shape=[jax.ShapeDtypeStruct(tok.shape(DV_ALL), F32), jax.ShapeDtypeStruct(s0.shape, F32)],
        scratch_shapes=[pltpu.VMEM((LANE, LANE), F32)],
        compiler_params=pltpu.CompilerParams(dimension_semantics=("arbitrary",), vmem_limit_bytes=VMEM_LIMIT),
        name="gla_scan_sample",
    )(proj, log_a, s0)

    h = pl.pallas_call(
        _gla_out_sample_kernel,
        grid=tok.grid,
        in_specs=[tok.x(D_MODEL), tok.mod(_mod_col(0, 1, 2)), tok.x(DV_ALL),
                  tok.x(DV_ALL, col=(2 * DK_ALL + DV_ALL) // DV_ALL), _resident(gout.shape), _resident(wout.shape)],
        out_specs=tok.x(D_MODEL),
        out_shape=jax.ShapeDtypeStruct(x.shape, F32),
        compiler_params=_params(),
        name="gla_out_sample",
    )(x, mod_all, oscan, proj, gout, wout)
    return h, s_out


def _kv_kernel(x_ref, sh_ref, sc_ref, g_ref, wkv_ref, wf_ref, bf_ref, gk_ref, *refs, prompt):
    if prompt:
        k_ref, v_ref, lf_ref, kb_ref, vb_ref, f_ref, ft_ref, carry_scr = refs
    else:
        k_ref, v_ref, lf_ref = refs
    x = x_ref[...]
    g_, r_, _ = x.shape
    n = g_ * r_
    u = _modulate(x, g_ref[...], _get_mod(sh_ref, prompt), _get_mod(sc_ref, prompt))
    u = u.reshape(n, D_MODEL).astype(BF16)
    kv = _dot(u, wkv_ref[...])
    k = kv[:, :D_MODEL]
    v = kv[:, D_MODEL:]
    kn = k * _head_inv_rms(k) * gk_ref[...]
    lf = _log_sigmoid(_dot(u, wf_ref[...]) + bf_ref[...])
    k_ref[...] = kn.reshape(g_, r_, D_MODEL)
    v_ref[...] = v.reshape(g_, r_, D_MODEL)
    lf_ref[...] = lf[:, :FOX_HEADS].reshape(g_, r_, FOX_HEADS)
    if prompt:
        kb_ref[...] = kn.astype(BF16).reshape(g_, r_, D_MODEL)
        vb_ref[...] = v.astype(BF16).reshape(g_, r_, D_MODEL)

        @pl.when(pl.program_id(1) == 0)
        def _():
            carry_scr[...] = jnp.zeros_like(carry_scr)

        tri = jnp.where(_lower_tri(n), 1.0, 0.0).astype(BF16)
        fsum = _split_dot(tri, lf, 3) + carry_scr[...]
        carry_scr[...] = fsum[n - 1:n, :]
        f_ref[...] = fsum[:, :FOX_HEADS].reshape(g_, r_, FOX_HEADS)
        ft_ref[0] = fsum.T[:FOX_HEADS, :]


def _kv(tok, x, mod_all, g, wkv, wf, bf, gk, name):
    prompt = tok.prompt
    out_specs = [tok.x(D_MODEL), tok.x(D_MODEL), tok.x(FOX_HEADS)]
    out_shape = [jax.ShapeDtypeStruct(tok.shape(D_MODEL), F32), jax.ShapeDtypeStruct(tok.shape(D_MODEL), F32),
                 jax.ShapeDtypeStruct(tok.shape(FOX_HEADS), F32)]
    scratch = []
    if prompt:
        out_specs += [tok.x(D_MODEL), tok.x(D_MODEL), tok.x(FOX_HEADS),
                      pl.BlockSpec((1, FOX_HEADS, TM_KV), lambda b, i: (b, 0, i))]
        out_shape += [jax.ShapeDtypeStruct(tok.shape(D_MODEL), BF16), jax.ShapeDtypeStruct(tok.shape(D_MODEL), BF16),
                      jax.ShapeDtypeStruct(tok.shape(FOX_HEADS), F32),
                      jax.ShapeDtypeStruct((BATCH, FOX_HEADS, SEQ), F32)]
        scratch = [pltpu.VMEM((1, LANE), F32)]
    return pl.pallas_call(
        functools.partial(_kv_kernel, prompt=prompt),
        grid=tok.grid,
        in_specs=[tok.x(D_MODEL), tok.mod(18), tok.mod(19), _resident((1, D_MODEL)), _resident(wkv.shape),
                  _resident(wf.shape), _resident(bf.shape), _resident(gk.shape)],
        out_specs=out_specs,
        out_shape=out_shape,
        scratch_shapes=scratch,
        compiler_params=_params(),
        name=name,
    )(x, mod_all, mod_all, g, wkv, wf, bf, gk)


def _fox_q(u, wqg_ref, gq_ref):
    qg = _dot(u, wqg_ref[...])
    q = qg[:, :D_MODEL]
    og = qg[:, D_MODEL:]
    qn = q * _head_inv_rms(q) * gq_ref[...] * (FOX_HD ** -0.5 * LOG2E)
    return qn, og


def _softmax_step(t, row_bias, m, l, acc, v):
    m_cur = jnp.max(t, axis=-1, keepdims=True)
    if row_bias is not None:
        m_cur = m_cur + row_bias
    m_new = jnp.maximum(m, m_cur)
    alpha = jnp.exp2(m - m_new)
    shift = -m_new if row_bias is None else row_bias - m_new
    p = jnp.exp2(t + jnp.tile(shift, (1, t.shape[1] // LANE)))
    l_new = alpha * l + jnp.sum(p, axis=-1, keepdims=True)
    pv = _dot(p.astype(BF16), v)
    acc_new = jnp.tile(alpha, (1, pv.shape[1] // LANE)) * acc + pv
    return m_new, l_new, acc_new


def _fox_prompt_kernel(x_ref, sh_ref, sc_ref, gt_ref, g_ref, wqg_ref, gq_ref, wo_ref, kb_ref, vb_ref, f_ref, ft_ref,
                       o_ref, qm_scr, og_scr, oatt_scr, m_scr, l_scr, acc_scr, fq_scr):
    i = pl.program_id(1)
    tq = x_ref.shape[1]
    x = x_ref[0]
    u = _modulate(x, g_ref[...], _get_mod(sh_ref, True), _get_mod(sc_ref, True)).astype(BF16)
    qn, og = _fox_q(u, wqg_ref, gq_ref)
    og_scr[...] = og
    first = lax.broadcasted_iota(jnp.int32, (tq, LANE), 1) < FOX_HD

    for pair in range(FOX_HEADS // 2):
        q2 = qn[:, pair * LANE:(pair + 1) * LANE]
        qm_scr[2 * pair] = jnp.where(first, q2, 0.0).astype(BF16)
        qm_scr[2 * pair + 1] = jnp.where(first, 0.0, q2).astype(BF16)

    fq_all = f_ref[0]
    causal = _lower_tri(tq)

    for grp in range(FOX_HEADS // FOX_GROUP):
        m_scr[...] = jnp.full_like(m_scr, NEG)
        l_scr[...] = jnp.zeros_like(l_scr)
        acc_scr[...] = jnp.zeros_like(acc_scr)
        for idx in range(FOX_GROUP):
            head = grp * FOX_GROUP + idx
            fq_scr[idx] = jnp.broadcast_to(fq_all[:, head:head + 1], (tq, LANE)) * LOG2E

        def step(j, diagonal, grp=grp):
            rows = pl.ds(pl.multiple_of(j * tq, tq), tq)
            for idx in range(FOX_GROUP):
                head = grp * FOX_GROUP + idx
                cols = slice((head // 2) * LANE, (head // 2 + 1) * LANE)
                fk = ft_ref[0, head, pl.ds(j, 1), :] * LOG2E
                t = _dot_nt(qm_scr[head], kb_ref[0, rows, cols]) - fk
                if diagonal:
                    t = jnp.where(causal, t, NEG)
                m, l, acc = _softmax_step(t, fq_scr[idx], m_scr[idx], l_scr[idx], acc_scr[idx],
                                          vb_ref[0, rows, cols])
                m_scr[idx] = m
                l_scr[idx] = l
                acc_scr[idx] = acc

        def off_diagonal(j, carry):
            step(j, False)
            return carry

        lax.fori_loop(0, i, off_diagonal, 0)
        step(i, True)
        for pp in range(FOX_GROUP // 2):
            pair = grp * (FOX_GROUP // 2) + pp
            o2 = jnp.where(first, acc_scr[2 * pp] / l_scr[2 * pp], acc_scr[2 * pp + 1] / l_scr[2 * pp + 1])
            oatt_scr[:, pair * LANE:(pair + 1) * LANE] = o2

    gated = (oatt_scr[...] * jax.nn.sigmoid(og_scr[...])).astype(BF16)
    o_ref[0] = x + _get_mod(gt_ref, True) * _dot(gated, wo_ref[...])


def _fox_prompt(x, mod_all, g, wqg, gq, wo, kb, vb, fsum, fsum_t):
    tok = _Tok(True, TQ_FOX)
    nk = SEQ // TQ_FOX
    seq_spec = pl.BlockSpec((1, SEQ, D_MODEL), lambda b, i: (b, 0, 0), pipeline_mode=pl.Buffered(1))
    return pl.pallas_call(
        _fox_prompt_kernel,
        grid=tok.grid,
        in_specs=[tok.x(D_MODEL), tok.mod(_mod_col(1, 1, 0)), tok.mod(_mod_col(1, 1, 1)), tok.mod(_mod_col(1, 1, 2)),
                  _resident((1, D_MODEL)), _resident(wqg.shape), _resident(gq.shape), _resident(wo.shape),
                  seq_spec, seq_spec, tok.x(FOX_HEADS),
                  pl.BlockSpec((1, FOX_HEADS, nk, TQ_FOX), lambda b, i: (b, 0, 0, 0))],
        out_specs=tok.x(D_MODEL),
        out_shape=jax.ShapeDtypeStruct(x.shape, F32),
        scratch_shapes=[pltpu.VMEM((FOX_HEADS, TQ_FOX, LANE), BF16), pltpu.VMEM((TQ_FOX, D_MODEL), F32),
                        pltpu.VMEM((TQ_FOX, D_MODEL), F32), pltpu.VMEM((FOX_GROUP, TQ_FOX, LANE), F32),
                        pltpu.VMEM((FOX_GROUP, TQ_FOX, LANE), F32), pltpu.VMEM((FOX_GROUP, TQ_FOX, LANE), F32),
                        pltpu.VMEM((FOX_GROUP, TQ_FOX, LANE), F32)],
        compiler_params=_params(),
        name="fox_prompt",
    )(x, mod_all, mod_all, mod_all, g, wqg, gq, wo, kb, vb, fsum, fsum_t.reshape(BATCH, FOX_HEADS, nk, TQ_FOX))


def _fox_q_sample_kernel(x_ref, sh_ref, sc_ref, g_ref, wqg_ref, gq_ref, q_ref, og_ref):
    x = x_ref[...]
    g_, r_, _ = x.shape
    u = _modulate(x, g_ref[...], _get_mod(sh_ref, False), _get_mod(sc_ref, False))
    qn, og = _fox_q(u.reshape(g_ * r_, D_MODEL).astype(BF16), wqg_ref, gq_ref)
    q_ref[...] = qn.reshape(g_, r_, D_MODEL)
    og_ref[...] = og.reshape(g_, r_, D_MODEL)


def _fox_out_sample_kernel(x_ref, gt_ref, oatt_ref, og_ref, wo_ref, o_ref):
    x = x_ref[...]
    g_, r_, _ = x.shape
    n = g_ * r_
    gated = (oatt_ref[...] * jax.nn.sigmoid(og_ref[...])).reshape(n, D_MODEL).astype(BF16)
    o_ref[...] = x + _get_mod(gt_ref, False) * _dot(gated, wo_ref[...]).reshape(g_, r_, D_MODEL)


def _gather_pages_kernel(pt_ref, k_ref, v_ref, ko_ref, vo_ref):
    del pt_ref
    ko_ref[0] = k_ref[0].astype(BF16).reshape(PAGE_SIZE, D_MODEL)
    vo_ref[0] = v_ref[0].astype(BF16).reshape(PAGE_SIZE, D_MODEL)


def _gather_pages(cache_k, cache_v, page_table):
    page_spec = pl.BlockSpec((1, PAGE_SIZE, FOX_HEADS, FOX_HD),
                             lambda b, p, pt: (pt[b, p], 0, 0, 0))
    out_spec = pl.BlockSpec((1, PAGE_SIZE, D_MODEL), lambda b, p, pt: (b, p, 0))
    out_shape = jax.ShapeDtypeStruct((DEC_BATCH, PAST_LEN, D_MODEL), BF16)
    return pl.pallas_call(
        _gather_pages_kernel,
        grid_spec=pltpu.PrefetchScalarGridSpec(
            num_scalar_prefetch=1, grid=(DEC_BATCH, N_PAGES),
            in_specs=[page_spec, page_spec], out_specs=[out_spec, out_spec]),
        out_shape=[out_shape, out_shape],
        compiler_params=_params(),
        name="gather_pages",
    )(page_table, cache_k, cache_v)


def _paged_kernel(pt_ref, q_ref, kn_ref, vn_ref, lnt_ref, k_ref, v_ref, *refs):
    del pt_ref
    l_pages = refs[:N_PAGES]
    o_ref = refs[N_PAGES]
    kn_scr, vn_scr = refs[N_PAGES + 1:]

    def page_rows(idx):
        return slice(idx * PAGE_SIZE, (idx + 1) * PAGE_SIZE)
    nrow = DEC_SEQ * FOX_HEADS

    hrow = lax.broadcasted_iota(jnp.int32, (FOX_HEADS, D_MODEL), 0)
    hcol = lax.broadcasted_iota(jnp.int32, (FOX_HEADS, D_MODEL), 1)
    head_mask = (hcol >> HD_SHIFT) == hrow
    q = q_ref[0]
    qbd = jnp.concatenate(
        [jnp.where(head_mask, jnp.broadcast_to(q[t:t + 1, :], (FOX_HEADS, D_MODEL)), 0.0) for t in range(DEC_SEQ)],
        axis=0).astype(BF16)

    lnt = lnt_ref[0]
    cn = [lnt[:, 0:1]]
    for t in range(1, DEC_SEQ):
        cn.append(cn[-1] + lnt[:, t:t + 1])
    cn_col = jnp.concatenate(cn, axis=0)

    ri = lax.broadcasted_iota(jnp.int32, (PAGE_SIZE, 2 * LANE), 0)
    ci = lax.broadcasted_iota(jnp.int32, (PAGE_SIZE, 2 * LANE), 1)
    suffix = jnp.where(((ci < PAGE_SIZE) & (ri > ci)) | (ci == PAGE_SIZE), 1.0, 0.0).astype(BF16)

    logits = [None] * N_PAGES
    tot = jnp.zeros((FOX_HEADS, 1), F32)
    for idx in reversed(range(N_PAGES)):
        sums = _split_dot_lhs(l_pages[idx][0], suffix, 2)
        bias16 = sums[:, :PAGE_SIZE] + tot
        tot = tot + sums[:, PAGE_SIZE:PAGE_SIZE + 1]
        bias = jnp.concatenate([bias16] * DEC_SEQ, axis=0) + cn_col
        logits[idx] = _dot_nt(qbd, k_ref[0, page_rows(idx), :]) + bias * LOG2E

    kn_scr[...] = jnp.zeros_like(kn_scr)
    vn_scr[...] = jnp.zeros_like(vn_scr)
    kn_scr[0:DEC_SEQ, :] = kn_ref[0]
    vn_scr[0:DEC_SEQ, :] = vn_ref[0]
    lane = lax.broadcasted_iota(jnp.int32, (nrow, LANE), 1)
    step_of_row = lax.broadcasted_iota(jnp.int32, (nrow, LANE), 0) >> HEAD_SHIFT
    cn_keys = jnp.zeros((nrow, LANE), F32)
    for t in range(DEC_SEQ):
        cn_keys = jnp.where(lane == t, jnp.concatenate([cn[t]] * DEC_SEQ, axis=0), cn_keys)
    s_new = _dot_nt(qbd, kn_scr[...].astype(BF16)) + (cn_col - cn_keys) * LOG2E
    logits.append(jnp.where(lane <= step_of_row, s_new, NEG))
    values = [v_ref[0, page_rows(idx), :] for idx in range(N_PAGES)] + [vn_scr[...].astype(BF16)]

    m = logits[0]
    for s in logits[1:]:
        m = jnp.maximum(m, s)
    m = jnp.broadcast_to(jnp.max(m, axis=-1, keepdims=True), (nrow, LANE))
    l = jnp.zeros((nrow, LANE), F32)
    acc = jnp.zeros((nrow, D_MODEL), F32)
    for s, v in zip(logits, values):
        p = jnp.exp2(s - m)
        l = l + p
        acc = acc + _dot(p.astype(BF16), v)
    l = jnp.broadcast_to(jnp.sum(l, axis=-1, keepdims=True), (nrow, LANE))
    out = acc / jnp.tile(l, (1, D_MODEL // LANE))
    for t in range(DEC_SEQ):
        blk = out[t * FOX_HEADS:(t + 1) * FOX_HEADS, :]
        o_ref[0, t:t + 1, :] = jnp.sum(jnp.where(head_mask, blk, 0.0), axis=0, keepdims=True)


def _paged_attend(q, k_new, v_new, lf_new_t, k_past, v_past, cache_lf_t, page_table):
    def page_spec(shape, idx):
        return pl.BlockSpec((1,) + shape, lambda b, pt: (pt[b, idx], 0, 0))

    seq_spec = pl.BlockSpec((1, DEC_SEQ, D_MODEL), lambda b, pt: (b, 0, 0))
    past_spec = pl.BlockSpec((1, PAST_LEN, D_MODEL), lambda b, pt: (b, 0, 0))
    in_specs = [seq_spec, seq_spec, seq_spec, pl.BlockSpec((1, FOX_HEADS, DEC_SEQ), lambda b, pt: (b, 0, 0)),
                past_spec, past_spec]
    in_specs += [page_spec((FOX_HEADS, PAGE_SIZE), i) for i in range(N_PAGES)]
    grid_spec = pltpu.PrefetchScalarGridSpec(
        num_scalar_prefetch=1,
        grid=(DEC_BATCH,),
        in_specs=in_specs,
        out_specs=seq_spec,
        scratch_shapes=[pltpu.VMEM((PAGE_SIZE, D_MODEL), F32), pltpu.VMEM((PAGE_SIZE, D_MODEL), F32)],
    )
    return pl.pallas_call(
        _paged_kernel,
        grid_spec=grid_spec,
        out_shape=jax.ShapeDtypeStruct((DEC_BATCH, DEC_SEQ, D_MODEL), F32),
        compiler_params=pltpu.CompilerParams(dimension_semantics=("arbitrary",), vmem_limit_bytes=VMEM_LIMIT),
        name="fox_paged",
    )(page_table, q, k_new, v_new, lf_new_t, k_past, v_past, *([cache_lf_t] * N_PAGES))


def _fox_sample(x, mod_all, g, wqg, gq, wo, k_new, v_new, lf_new, cache_k, cache_v, cache_logf, page_table):
    tok = _Tok(False, 0)
    qn, og = pl.pallas_call(
        _fox_q_sample_kernel,
        grid=tok.grid,
        in_specs=[tok.x(D_MODEL), tok.mod(_mod_col(1, 1, 0)), tok.mod(_mod_col(1, 1, 1)), _resident((1, D_MODEL)),
                  _resident(wqg.shape), _resident(gq.shape)],
        out_specs=[tok.x(D_MODEL), tok.x(D_MODEL)],
        out_shape=[jax.ShapeDtypeStruct(x.shape, F32), jax.ShapeDtypeStruct(x.shape, F32)],
        compiler_params=_params(),
        name="fox_q_sample",
    )(x, mod_all, mod_all, g, wqg, gq)

    k_past, v_past = _gather_pages(cache_k, cache_v, page_table)
    oatt = _paged_attend(
        qn.transpose(1, 0, 2), k_new.transpose(1, 0, 2), v_new.transpose(1, 0, 2), lf_new.transpose(1, 2, 0),
        k_past, v_past, cache_logf.transpose(0, 2, 1), page_table)

    return pl.pallas_call(
        _fox_out_sample_kernel,
        grid=tok.grid,
        in_specs=[tok.x(D_MODEL), tok.mod(_mod_col(1, 1, 2)), tok.x(D_MODEL), tok.x(D_MODEL), _resident(wo.shape)],
        out_specs=tok.x(D_MODEL),
        out_shape=jax.ShapeDtypeStruct(x.shape, F32),
        compiler_params=_params(),
        name="fox_out_sample",
    )(x, mod_all, oatt.transpose(1, 0, 2), og, wo)


def kernel(x_prompt, x_sample, state_gla, cache_k, cache_v, cache_logf, page_table, c_prompt, c_sample, w_ada, b_ada, g_norm, w_ffn_up, w_ffn_down, gla_w_in, gla_w_gate2, gla_b_gate, gla_g_out, gla_w_out, w_ada_kv, b_ada_kv, g_kv, w_kvf, b_f, g_k, fox_w_qg, fox_g_q, fox_w_o):
    w_mod = jnp.concatenate([w_ada[0], w_ada[1], w_ada_kv], axis=1).astype(BF16)
    b_mod = jnp.concatenate([b_ada[0], b_ada[1], b_ada_kv], axis=0)[None, :]
    c_all = jnp.concatenate([c_sample, c_prompt, jnp.zeros((MOD_ROWS - DEC_BATCH - BATCH, D_MODEL), F32)], axis=0)
    wu = w_ffn_up.astype(BF16)
    wd = w_ffn_down.astype(BF16)
    w_in = gla_w_in[0]
    wqkvr = w_in[:, :QKVR].astype(BF16)
    wglr = jnp.pad(w_in[:, QKVR:], ((0, 0), (0, LANE - GLA_RANK))).astype(BF16)
    wg2 = jnp.pad(gla_w_gate2[0], ((0, LANE - GLA_RANK), (0, 0))).astype(BF16)
    bg = gla_b_gate[0][None, :]
    gout = gla_g_out[0][None, :]
    wout = gla_w_out[0].astype(BF16)
    wkv = w_kvf[:, :2 * D_MODEL].astype(BF16)
    wf = jnp.pad(w_kvf[:, 2 * D_MODEL:], ((0, 0), (0, LANE - FOX_HEADS))).astype(BF16)
    bf = jnp.pad(b_f, (0, LANE - FOX_HEADS))[None, :]
    gk = jnp.tile(g_k, FOX_HEADS)[None, :]
    wqg = fox_w_qg[0].astype(BF16)
    gq = jnp.tile(fox_g_q[0], FOX_HEADS)[None, :]
    wo = fox_w_o[0].astype(BF16)
    gkv = g_kv[None, :]

    def gn(layer, sub):
        return g_norm[layer, sub][None, :]

    mod_all = _ada(c_all, w_mod, b_mod)

    tok = _Tok(True, TM_FFN)
    h = _ffn(tok, x_prompt, mod_all, 0, 0, gn(0, 0), wu, wd,"ffn_p00")
    s0 = jnp.zeros((BATCH, GLA_HEADS, GLA_DK, GLA_DV), F32)
    h, sg_prompt = _gla_prompt(h, mod_all, gn(0, 1), wqkvr, wglr, wg2, bg, gout, wout, s0)
    h = _ffn(tok, h, mod_all, 0, 2, gn(0, 2), wu, wd,"ffn_p02")
    k_p, v_p, lf_p, kb, vb, fsum, fsum_t = _kv(_Tok(True, TM_KV), h, mod_all, gkv, wkv, wf, bf, gk, "kv_prompt")
    h = _ffn(tok, h, mod_all, 1, 0, gn(1, 0), wu, wd,"ffn_p10")
    h = _fox_prompt(h, mod_all, gn(1, 1), wqg, gq, wo, kb, vb, fsum, fsum_t)
    y_prompt = _ffn(tok, h, mod_all, 1, 2, gn(1, 2), wu, wd,"ffn_p12")

    tok = _Tok(False, 0)
    hs = x_sample.transpose(1, 0, 2)
    hs = _ffn(tok, hs, mod_all, 0, 0, gn(0, 0), wu, wd,"ffn_s00")
    hs, sg_sample = _gla_sample(hs, mod_all, gn(0, 1), wqkvr, wglr, wg2, bg, gout, wout, state_gla[0])
    hs = _ffn(tok, hs, mod_all, 0, 2, gn(0, 2), wu, wd,"ffn_s02")
    k_s, v_s, lf_s = _kv(tok, hs, mod_all, gkv, wkv, wf, bf, gk, "kv_sample")
    hs = _ffn(tok, hs, mod_all, 1, 0, gn(1, 0), wu, wd,"ffn_s10")
    hs = _fox_sample(hs, mod_all, gn(1, 1), wqg, gq, wo, k_s, v_s, lf_s, cache_k, cache_v, cache_logf, page_table)
    hs = _ffn(tok, hs, mod_all, 1, 2, gn(1, 2), wu, wd,"ffn_s12")
    y_sample = hs.transpose(1, 0, 2)

    def heads(t, lead):
        return t.reshape(lead + (FOX_HEADS, FOX_HD))

    return (y_prompt, y_sample, sg_prompt[None],
            heads(k_p, (BATCH, SEQ)), heads(v_p, (BATCH, SEQ)), lf_p,
            sg_sample[None],
            heads(k_s.transpose(1, 0, 2), (DEC_BATCH, DEC_SEQ)), heads(v_s.transpose(1, 0, 2), (DEC_BATCH, DEC_SEQ)),
            lf_s.transpose(1, 0, 2))
```

```python
import functools

import jax
import jax.numpy as jnp
from jax import lax
from jax.experimental import pallas as pl
from jax.experimental.pallas import tpu as pltpu

F32 = jnp.float32
BF16 = jnp.bfloat16

D_MODEL = 1024
BATCH = 4
SEQ = 4096
DEC_BATCH = 128
DEC_SEQ = 4
PAST_LEN = 2048
PAGE_SIZE = 128
N_PAGES = PAST_LEN // PAGE_SIZE
GLA_HEADS = 4
GLA_DK = 128
GLA_DV = 256
GLA_RANK = 16
GLA_TAU = 16.0
FOX_HEADS = 16
FOX_HD = 64
D_FF = 2816
NORM_EPS = 1e-6
HD_SHIFT = FOX_HD.bit_length() - 1
HEAD_SHIFT = FOX_HEADS.bit_length() - 1

DK_ALL = GLA_HEADS * GLA_DK
DV_ALL = GLA_HEADS * GLA_DV
QKVR = 2 * DK_ALL + 2 * DV_ALL
LANE = 128
SUBLANE = 8

NEG = -1e30
LOG2E = 1.4426950408889634
VMEM_LIMIT = 56 * 1024 * 1024

TM_FFN = 512
FF_CHUNK = 1408
TM_GLA = 512
GLA_CHUNK = 128
TM_KV = 512
TQ_FOX = 512
FOX_GROUP = 4
ADA_TN = 2048
SCAN_BS = 8

N_MOD_BLOCKS = 2 * 9 + 2
MOD_ROWS = DEC_BATCH + SUBLANE
PROMPT_ROW_BLOCK = DEC_BATCH // SUBLANE


def _mod_col(layer, sub, kind):
    return layer * 9 + sub * 3 + kind


def _dot(a, b):
    return jnp.dot(a, b, preferred_element_type=F32)


def _dot_nt(a, b):
    return lax.dot_general(a, b, (((1,), (1,)), ((), ())), preferred_element_type=F32)


def _dot_tn(a, b):
    return lax.dot_general(a, b, (((0,), (0,)), ((), ())), preferred_element_type=F32)


def _split_dot(a_bf, x, terms):
    out = None
    r = x
    for _ in range(terms):
        p = r.astype(BF16)
        r = r - p.astype(F32)
        d = _dot(a_bf, p)
        out = d if out is None else out + d
    return out


def _split_dot_lhs(x, b_bf, terms):
    out = None
    r = x
    for _ in range(terms):
        p = r.astype(BF16)
        r = r - p.astype(F32)
        d = _dot(p, b_bf)
        out = d if out is None else out + d
    return out


def _log_sigmoid(x):
    return jnp.minimum(x, 0.0) - jnp.log1p(jnp.exp(-jnp.abs(x)))


def _rms(x):
    return x * lax.rsqrt(jnp.mean(x * x, axis=-1, keepdims=True) + NORM_EPS)


def _modulate(x, g, shift, scale):
    return (_rms(x) * g) * (1.0 + scale) + shift


def _get_mod(ref, prompt):
    if prompt:
        return ref[pl.ds(pl.program_id(0), 1), :]
    return ref[...]


def _lower_tri(n, strict=False):
    r = lax.broadcasted_iota(jnp.int32, (n, n), 0)
    c = lax.broadcasted_iota(jnp.int32, (n, n), 1)
    return (r > c) if strict else (r >= c)


def _head_indicator(transposed):
    if transposed:
        h = lax.broadcasted_iota(jnp.int32, (LANE, D_MODEL), 0)
        c = lax.broadcasted_iota(jnp.int32, (LANE, D_MODEL), 1)
    else:
        c = lax.broadcasted_iota(jnp.int32, (D_MODEL, LANE), 0)
        h = lax.broadcasted_iota(jnp.int32, (D_MODEL, LANE), 1)
    return jnp.where((c >> HD_SHIFT) == h, 1.0, 0.0).astype(BF16)


def _head_inv_rms(x):
    ss = _split_dot_lhs(x * x, _head_indicator(False), 2)
    inv = lax.rsqrt(ss * (1.0 / FOX_HD) + NORM_EPS)
    return _split_dot_lhs(inv, _head_indicator(True), 2)


class _Tok:
    def __init__(self, prompt, tm):
        self.prompt = prompt
        if prompt:
            self.grid = (BATCH, SEQ // tm)
            self.rows = (1, tm)
            self.mod_block = (SUBLANE, D_MODEL)
            self.mod_row = PROMPT_ROW_BLOCK
        else:
            self.grid = (1, 1)
            self.rows = (DEC_SEQ, DEC_BATCH)
            self.mod_block = (DEC_BATCH, D_MODEL)
            self.mod_row = 0

    def x(self, n, col=0):
        if self.prompt:
            return pl.BlockSpec(self.rows + (n,), lambda b, i: (b, i, col))
        return pl.BlockSpec(self.rows + (n,), lambda b, i: (0, 0, col))

    def mod(self, col):
        row = self.mod_row
        return pl.BlockSpec(self.mod_block, lambda b, i: (row, col))

    def shape(self, n):
        return (BATCH, SEQ, n) if self.prompt else (DEC_SEQ, DEC_BATCH, n)


def _resident(shape):
    nd = len(shape)
    return pl.BlockSpec(shape, lambda *_: (0,) * nd, pipeline_mode=pl.Buffered(1))


def _resident_at(shape, lead):
    nd = len(shape) - len(lead)
    block = (None,) * len(lead) + tuple(shape[len(lead):])
    return pl.BlockSpec(block, lambda *_: tuple(lead) + (0,) * nd, pipeline_mode=pl.Buffered(1))


def _params():
    return pltpu.CompilerParams(dimension_semantics=("arbitrary", "arbitrary"), vmem_limit_bytes=VMEM_LIMIT)


def _ada_kernel(c_ref, w_ref, b_ref, o_ref):
    sc = jax.nn.silu(c_ref[...]).astype(BF16)
    o_ref[...] = _dot(sc, w_ref[...]) + b_ref[...]


def _ada(c_all, w_all, b_all):
    n = w_all.shape[1]
    return pl.pallas_call(
        _ada_kernel,
        grid=(n // ADA_TN,),
        in_specs=[pl.BlockSpec((MOD_ROWS, D_MODEL), lambda j: (0, 0)),
                  pl.BlockSpec((D_MODEL, ADA_TN), lambda j: (0, j)),
                  pl.BlockSpec((1, ADA_TN), lambda j: (0, j))],
        out_specs=pl.BlockSpec((MOD_ROWS, ADA_TN), lambda j: (0, j)),
        out_shape=jax.ShapeDtypeStruct((MOD_ROWS, n), F32),
        compiler_params=pltpu.CompilerParams(dimension_semantics=("arbitrary",), vmem_limit_bytes=VMEM_LIMIT),
        name="ada_mod",
    )(c_all, w_all, b_all)


def _ffn_kernel(x_ref, sh_ref, sc_ref, gt_ref, g_ref, wu_ref, wd_ref, o_ref, *, prompt):
    x = x_ref[...]
    g_, r_, _ = x.shape
    n = g_ * r_
    u = _modulate(x, g_ref[...], _get_mod(sh_ref, prompt), _get_mod(sc_ref, prompt))
    u = u.reshape(n, D_MODEL).astype(BF16)
    acc = None
    for c in range(D_FF // FF_CHUNK):
        lo = c * FF_CHUNK
        a = _dot(u, wu_ref[:, lo:lo + FF_CHUNK])
        b = _dot(u, wu_ref[:, D_FF + lo:D_FF + lo + FF_CHUNK])
        gated = (jax.nn.silu(a) * b).astype(BF16)
        part = _dot(gated, wd_ref[lo:lo + FF_CHUNK, :])
        acc = part if acc is None else acc + part
    o_ref[...] = x + (0.5 * _get_mod(gt_ref, prompt)) * acc.reshape(g_, r_, D_MODEL)


def _ffn(tok, x, mod_all, layer, sub, g, wu, wd, name):
    which = (layer, sub // 2)
    return pl.pallas_call(
        functools.partial(_ffn_kernel, prompt=tok.prompt),
        grid=tok.grid,
        in_specs=[tok.x(D_MODEL),
                  tok.mod(_mod_col(layer, sub, 0)), tok.mod(_mod_col(layer, sub, 1)), tok.mod(_mod_col(layer, sub, 2)),
                  _resident((1, D_MODEL)), _resident_at(wu.shape, which), _resident_at(wd.shape, which)],
        out_specs=tok.x(D_MODEL),
        out_shape=jax.ShapeDtypeStruct(x.shape, F32),
        compiler_params=_params(),
        name=name,
    )(x, mod_all, mod_all, mod_all, g, wu, wd)


def _gla_in(u, wqkvr_ref, wglr_ref, wg2_ref, bg_ref):
    proj = _dot(u, wqkvr_ref[...])
    glr = _dot(u, wglr_ref[...])
    xg = _dot(glr.astype(BF16), wg2_ref[...]) + bg_ref[...]
    return proj, _log_sigmoid(xg) * (1.0 / GLA_TAU)


def _gla_out(o, r, gout, wout_ref):
    heads = []
    for h in range(GLA_HEADS):
        oh = o[:, h * GLA_DV:(h + 1) * GLA_DV]
        heads.append(_rms(oh) * gout)
    y = (jnp.concatenate(heads, axis=-1) * jax.nn.silu(r)).astype(BF16)
    return _dot(y, wout_ref[...])


def _gla_prompt_kernel(x_ref, sh_ref, sc_ref, gt_ref, g_ref, wqkvr_ref, wglr_ref, wg2_ref, bg_ref, gout_ref,
                       wout_ref, s0_ref, o_ref, sout_ref, proj_scr, la_scr, oscan_scr, st_scr):
    i = pl.program_id(1)
    tm = x_ref.shape[1]
    x = x_ref[0]
    u = _modulate(x, g_ref[...], _get_mod(sh_ref, True), _get_mod(sc_ref, True)).astype(BF16)
    proj, log_a = _gla_in(u, wqkvr_ref, wglr_ref, wg2_ref, bg_ref)
    proj_scr[...] = proj
    la_scr[...] = log_a

    @pl.when(i == 0)
    def _():
        for h in range(GLA_HEADS):
            st_scr[h] = s0_ref[0, h].T

    tri = jnp.where(_lower_tri(GLA_CHUNK), 1.0, 0.0).astype(BF16)
    causal = _lower_tri(GLA_CHUNK)
    qscale = GLA_DK ** -0.5

    def chunk(c, carry):
        rows = pl.ds(pl.multiple_of(c * GLA_CHUNK, GLA_CHUNK), GLA_CHUNK)
        bcum = _split_dot(tri, la_scr[rows, :], 3)
        blast = bcum[GLA_CHUNK - 1:GLA_CHUNK, :]
        q = proj_scr[rows, 0:DK_ALL] * qscale
        k = proj_scr[rows, DK_ALL:2 * DK_ALL]
        qe = (q * jnp.exp(bcum)).astype(BF16)
        kinv = (k * jnp.exp(-bcum)).astype(BF16)
        kd = (k * jnp.exp(blast - bcum)).astype(BF16)
        elast = jnp.exp(blast)
        for h in range(GLA_HEADS):
            ks = slice(h * GLA_DK, (h + 1) * GLA_DK)
            vlo = 2 * DK_ALL + h * GLA_DV
            v = proj_scr[rows, vlo:vlo + GLA_DV].astype(BF16)
            st = st_scr[h]
            att = jnp.where(causal, _dot_nt(qe[:, ks], kinv[:, ks]), 0.0).astype(BF16)
            oscan_scr[rows, h * GLA_DV:(h + 1) * GLA_DV] = _dot_nt(qe[:, ks], st.astype(BF16)) + _dot(att, v)
            st_scr[h] = st * elast[:, ks] + _dot_tn(v, kd[:, ks])
        return carry

    lax.fori_loop(0, tm // GLA_CHUNK, chunk, 0)

    r = proj_scr[:, 2 * DK_ALL + DV_ALL:QKVR]
    mix = _gla_out(oscan_scr[...], r, gout_ref[...], wout_ref)
    o_ref[0] = x + _get_mod(gt_ref, True) * mix

    @pl.when(i == pl.num_programs(1) - 1)
    def _():
        for h in range(GLA_HEADS):
            sout_ref[0, h] = st_scr[h].T


def _gla_prompt(x, mod_all, g, wqkvr, wglr, wg2, bg, gout, wout, s0):
    tok = _Tok(True, TM_GLA)
    state_spec = pl.BlockSpec((1, GLA_HEADS, GLA_DK, GLA_DV), lambda b, i: (b, 0, 0, 0))
    return pl.pallas_call(
        _gla_prompt_kernel,
        grid=tok.grid,
        in_specs=[tok.x(D_MODEL), tok.mod(_mod_col(0, 1, 0)), tok.mod(_mod_col(0, 1, 1)), tok.mod(_mod_col(0, 1, 2)),
                  _resident((1, D_MODEL)), _resident(wqkvr.shape), _resident(wglr.shape), _resident(wg2.shape),
                  _resident(bg.shape), _resident(gout.shape), _resident(wout.shape), state_spec],
        out_specs=[tok.x(D_MODEL), state_spec],
        out_shape=[jax.ShapeDtypeStruct(x.shape, F32), jax.ShapeDtypeStruct(s0.shape, F32)],
        scratch_shapes=[pltpu.VMEM((TM_GLA, QKVR), F32), pltpu.VMEM((TM_GLA, DK_ALL), F32),
                        pltpu.VMEM((TM_GLA, DV_ALL), F32), pltpu.VMEM((GLA_HEADS, GLA_DV, GLA_DK), F32)],
        compiler_params=_params(),
        name="gla_prompt",
    )(x, mod_all, mod_all, mod_all, g, wqkvr, wglr, wg2, bg, gout, wout, s0)


def _gla_in_sample_kernel(x_ref, sh_ref, sc_ref, g_ref, wqkvr_ref, wglr_ref, wg2_ref, bg_ref, proj_ref, la_ref):
    x = x_ref[...]
    g_, r_, _ = x.shape
    u = _modulate(x, g_ref[...], _get_mod(sh_ref, False), _get_mod(sc_ref, False))
    u = u.reshape(g_ * r_, D_MODEL).astype(BF16)
    proj, log_a = _gla_in(u, wqkvr_ref, wglr_ref, wg2_ref, bg_ref)
    proj_ref[...] = proj.reshape(g_, r_, QKVR)
    la_ref[...] = log_a.reshape(g_, r_, DK_ALL)


def _gla_scan_sample_kernel(proj_ref, la_ref, s0_ref, o_ref, sout_ref, xt_scr):
    qscale = GLA_DK ** -0.5
    n_kind = DEC_SEQ * GLA_HEADS
    xt_scr[...] = jnp.zeros_like(xt_scr)

    def seq(j, carry):
        for t in range(DEC_SEQ):
            a_t = jnp.exp(la_ref[t, pl.ds(j, 1), :])
            q_t = proj_ref[t, pl.ds(j, 1), 0:DK_ALL] * qscale
            k_t = proj_ref[t, pl.ds(j, 1), DK_ALL:2 * DK_ALL]
            for h in range(GLA_HEADS):
                ks = slice(h * GLA_DK, (h + 1) * GLA_DK)
                row = h * DEC_SEQ + t
                xt_scr[row:row + 1, :] = a_t[:, ks]
                xt_scr[n_kind + row:n_kind + row + 1, :] = q_t[:, ks]
                xt_scr[2 * n_kind + row:2 * n_kind + row + 1, :] = k_t[:, ks]
        xt = xt_scr[...].T
        for h in range(GLA_HEADS):
            s = s0_ref[j, h]
            for t in range(DEC_SEQ):
                row = h * DEC_SEQ + t
                a_c = xt[:, row:row + 1]
                q_c = xt[:, n_kind + row:n_kind + row + 1]
                k_c = xt[:, 2 * n_kind + row:2 * n_kind + row + 1]
                vlo = 2 * DK_ALL + h * GLA_DV
                v_t = proj_ref[t, pl.ds(j, 1), vlo:vlo + GLA_DV]
                s = a_c * s + k_c * v_t
                o_ref[t, pl.ds(j, 1), h * GLA_DV:(h + 1) * GLA_DV] = jnp.sum(q_c * s, axis=0, keepdims=True)
            sout_ref[j, h] = s
        return carry

    lax.fori_loop(0, SCAN_BS, seq, 0)


def _gla_out_sample_kernel(x_ref, gt_ref, oscan_ref, r_ref, gout_ref, wout_ref, o_ref):
    x = x_ref[...]
    g_, r_, _ = x.shape
    n = g_ * r_
    mix = _gla_out(oscan_ref[...].reshape(n, DV_ALL), r_ref[...].reshape(n, DV_ALL), gout_ref[...], wout_ref)
    o_ref[...] = x + _get_mod(gt_ref, False) * mix.reshape(g_, r_, D_MODEL)


def _gla_sample(x, mod_all, g, wqkvr, wglr, wg2, bg, gout, wout, s0):
    tok = _Tok(False, 0)
    proj, log_a = pl.pallas_call(
        _gla_in_sample_kernel,
        grid=tok.grid,
        in_specs=[tok.x(D_MODEL), tok.mod(_mod_col(0, 1, 0)), tok.mod(_mod_col(0, 1, 1)),
                  _resident((1, D_MODEL)), _resident(wqkvr.shape), _resident(wglr.shape), _resident(wg2.shape),
                  _resident(bg.shape)],
        out_specs=[tok.x(QKVR), tok.x(DK_ALL)],
        out_shape=[jax.ShapeDtypeStruct(tok.shape(QKVR), F32), jax.ShapeDtypeStruct(tok.shape(DK_ALL), F32)],
        compiler_params=_params(),
        name="gla_in_sample",
    )(x, mod_all, mod_all, g, wqkvr, wglr, wg2, bg)

    state_spec = pl.BlockSpec((SCAN_BS, GLA_HEADS, GLA_DK, GLA_DV), lambda j: (j, 0, 0, 0))
    oscan, s_out = pl.pallas_call(
        _gla_scan_sample_kernel,
        grid=(DEC_BATCH // SCAN_BS,),
        in_specs=[pl.BlockSpec((DEC_SEQ, SCAN_BS, QKVR), lambda j: (0, j, 0)),
                  pl.BlockSpec((DEC_SEQ, SCAN_BS, DK_ALL), lambda j: (0, j, 0)),
                  state_spec],
        out_specs=[pl.BlockSpec((DEC_SEQ, SCAN_BS, DV_ALL), lambda j: (0, j, 0)), state_spec],
        out_shape=[jax.ShapeDtypeStruct(tok.shape(DV_ALL), F32), jax.ShapeDtypeStruct(s0.shape, F32)],
        scratch_shapes=[pltpu.VMEM((LANE, LANE), F32)],
        compiler_params=pltpu.CompilerParams(dimension_semantics=("arbitrary",), vmem_limit_bytes=VMEM_LIMIT),
        name="gla_scan_sample",
    )(proj, log_a, s0)

    h = pl.pallas_call(
        _gla_out_sample_kernel,
        grid=tok.grid,
        in_specs=[tok.x(D_MODEL), tok.mod(_mod_col(0, 1, 2)), tok.x(DV_ALL),
                  tok.x(DV_ALL, col=(2 * DK_ALL + DV_ALL) // DV_ALL), _resident(gout.shape), _resident(wout.shape)],
        out_specs=tok.x(D_MODEL),
        out_shape=jax.ShapeDtypeStruct(x.shape, F32),
        compiler_params=_params(),
        name="gla_out_sample",
    )(x, mod_all, oscan, proj, gout, wout)
    return h, s_out


def _kv_kernel(x_ref, sh_ref, sc_ref, g_ref, wkv_ref, wf_ref, bf_ref, gk_ref, *refs, prompt):
    if prompt:
        k_ref, v_ref, lf_ref, kb_ref, vb_ref, f_ref, ft_ref, carry_scr = refs
    else:
        k_ref, v_ref, lf_ref = refs
    x = x_ref[...]
    g_, r_, _ = x.shape
    n = g_ * r_
    u = _modulate(x, g_ref[...], _get_mod(sh_ref, prompt), _get_mod(sc_ref, prompt))
    u = u.reshape(n, D_MODEL).astype(BF16)
    kv = _dot(u, wkv_ref[...])
    k = kv[:, :D_MODEL]
    v = kv[:, D_MODEL:]
    kn = k * _head_inv_rms(k) * gk_ref[...]
    lf = _log_sigmoid(_dot(u, wf_ref[...]) + bf_ref[...])
    k_ref[...] = kn.reshape(g_, r_, D_MODEL)
    v_ref[...] = v.reshape(g_, r_, D_MODEL)
    lf_ref[...] = lf[:, :FOX_HEADS].reshape(g_, r_, FOX_HEADS)
    if prompt:
        kb_ref[...] = kn.astype(BF16).reshape(g_, r_, D_MODEL)
        vb_ref[...] = v.astype(BF16).reshape(g_, r_, D_MODEL)

        @pl.when(pl.program_id(1) == 0)
        def _():
            carry_scr[...] = jnp.zeros_like(carry_scr)

        tri = jnp.where(_lower_tri(n), 1.0, 0.0).astype(BF16)
        fsum = _split_dot(tri, lf, 3) + carry_scr[...]
        carry_scr[...] = fsum[n - 1:n, :]
        f_ref[...] = fsum[:, :FOX_HEADS].reshape(g_, r_, FOX_HEADS)
        ft_ref[0] = fsum.T[:FOX_HEADS, :]


def _kv(tok, x, mod_all, g, wkv, wf, bf, gk, name):
    prompt = tok.prompt
    out_specs = [tok.x(D_MODEL), tok.x(D_MODEL), tok.x(FOX_HEADS)]
    out_shape = [jax.ShapeDtypeStruct(tok.shape(D_MODEL), F32), jax.ShapeDtypeStruct(tok.shape(D_MODEL), F32),
                 jax.ShapeDtypeStruct(tok.shape(FOX_HEADS), F32)]
    scratch = []
    if prompt:
        out_specs += [tok.x(D_MODEL), tok.x(D_MODEL), tok.x(FOX_HEADS),
                      pl.BlockSpec((1, FOX_HEADS, TM_KV), lambda b, i: (b, 0, i))]
        out_shape += [jax.ShapeDtypeStruct(tok.shape(D_MODEL), BF16), jax.ShapeDtypeStruct(tok.shape(D_MODEL), BF16),
                      jax.ShapeDtypeStruct(tok.shape(FOX_HEADS), F32),
                      jax.ShapeDtypeStruct((BATCH, FOX_HEADS, SEQ), F32)]
        scratch = [pltpu.VMEM((1, LANE), F32)]
    return pl.pallas_call(
        functools.partial(_kv_kernel, prompt=prompt),
        grid=tok.grid,
        in_specs=[tok.x(D_MODEL), tok.mod(18), tok.mod(19), _resident((1, D_MODEL)), _resident(wkv.shape),
                  _resident(wf.shape), _resident(bf.shape), _resident(gk.shape)],
        out_specs=out_specs,
        out_shape=out_shape,
        scratch_shapes=scratch,
        compiler_params=_params(),
        name=name,
    )(x, mod_all, mod_all, g, wkv, wf, bf, gk)


def _fox_q(u, wqg_ref, gq_ref):
    qg = _dot(u, wqg_ref[...])
    q = qg[:, :D_MODEL]
    og = qg[:, D_MODEL:]
    qn = q * _head_inv_rms(q) * gq_ref[...] * (FOX_HD ** -0.5 * LOG2E)
    return qn, og


def _softmax_step(t, row_bias, m, l, acc, v):
    m_cur = jnp.max(t, axis=-1, keepdims=True)
    if row_bias is not None:
        m_cur = m_cur + row_bias
    m_new = jnp.maximum(m, m_cur)
    alpha = jnp.exp2(m - m_new)
    shift = -m_new if row_bias is None else row_bias - m_new
    p = jnp.exp2(t + jnp.tile(shift, (1, t.shape[1] // LANE)))
    l_new = alpha * l + jnp.sum(p, axis=-1, keepdims=True)
    pv = _dot(p.astype(BF16), v)
    acc_new = jnp.tile(alpha, (1, pv.shape[1] // LANE)) * acc + pv
    return m_new, l_new, acc_new


def _fox_prompt_kernel(x_ref, sh_ref, sc_ref, gt_ref, g_ref, wqg_ref, gq_ref, wo_ref, kb_ref, vb_ref, f_ref, ft_ref,
                       o_ref, qm_scr, og_scr, oatt_scr, m_scr, l_scr, acc_scr, fq_scr):
    i = pl.program_id(1)
    tq = x_ref.shape[1]
    x = x_ref[0]
    u = _modulate(x, g_ref[...], _get_mod(sh_ref, True), _get_mod(sc_ref, True)).astype(BF16)
    qn, og = _fox_q(u, wqg_ref, gq_ref)
    og_scr[...] = og
    first = lax.broadcasted_iota(jnp.int32, (tq, LANE), 1) < FOX_HD

    for pair in range(FOX_HEADS // 2):
        q2 = qn[:, pair * LANE:(pair + 1) * LANE]
        qm_scr[2 * pair] = jnp.where(first, q2, 0.0).astype(BF16)
        qm_scr[2 * pair + 1] = jnp.where(first, 0.0, q2).astype(BF16)

    fq_all = f_ref[0]
    causal = _lower_tri(tq)

    for grp in range(FOX_HEADS // FOX_GROUP):
        m_scr[...] = jnp.full_like(m_scr, NEG)
        l_scr[...] = jnp.zeros_like(l_scr)
        acc_scr[...] = jnp.zeros_like(acc_scr)
        for idx in range(FOX_GROUP):
            head = grp * FOX_GROUP + idx
            fq_scr[idx] = jnp.broadcast_to(fq_all[:, head:head + 1], (tq, LANE)) * LOG2E

        def step(j, diagonal, grp=grp):
            rows = pl.ds(pl.multiple_of(j * tq, tq), tq)
            for idx in range(FOX_GROUP):
                head = grp * FOX_GROUP + idx
                cols = slice((head // 2) * LANE, (head // 2 + 1) * LANE)
                fk = ft_ref[0, head, pl.ds(j, 1), :] * LOG2E
                t = _dot_nt(qm_scr[head], kb_ref[0, rows, cols]) - fk
                if diagonal:
                    t = jnp.where(causal, t, NEG)
                m, l, acc = _softmax_step(t, fq_scr[idx], m_scr[idx], l_scr[idx], acc_scr[idx],
                                          vb_ref[0, rows, cols])
                m_scr[idx] = m
                l_scr[idx] = l
                acc_scr[idx] = acc

        def off_diagonal(j, carry):
            step(j, False)
            return carry

        lax.fori_loop(0, i, off_diagonal, 0)
        step(i, True)
        for pp in range(FOX_GROUP // 2):
            pair = grp * (FOX_GROUP // 2) + pp
            o2 = jnp.where(first, acc_scr[2 * pp] / l_scr[2 * pp], acc_scr[2 * pp + 1] / l_scr[2 * pp + 1])
            oatt_scr[:, pair * LANE:(pair + 1) * LANE] = o2

    gated = (oatt_scr[...] * jax.nn.sigmoid(og_scr[...])).astype(BF16)
    o_ref[0] = x + _get_mod(gt_ref, True) * _dot(gated, wo_ref[...])


def _fox_prompt(x, mod_all, g, wqg, gq, wo, kb, vb, fsum, fsum_t):
    tok = _Tok(True, TQ_FOX)
    nk = SEQ // TQ_FOX
    seq_spec = pl.BlockSpec((1, SEQ, D_MODEL), lambda b, i: (b, 0, 0), pipeline_mode=pl.Buffered(1))
    return pl.pallas_call(
        _fox_prompt_kernel,
        grid=tok.grid,
        in_specs=[tok.x(D_MODEL), tok.mod(_mod_col(1, 1, 0)), tok.mod(_mod_col(1, 1, 1)), tok.mod(_mod_col(1, 1, 2)),
                  _resident((1, D_MODEL)), _resident(wqg.shape), _resident(gq.shape), _resident(wo.shape),
                  seq_spec, seq_spec, tok.x(FOX_HEADS),
                  pl.BlockSpec((1, FOX_HEADS, nk, TQ_FOX), lambda b, i: (b, 0, 0, 0))],
        out_specs=tok.x(D_MODEL),
        out_shape=jax.ShapeDtypeStruct(x.shape, F32),
        scratch_shapes=[pltpu.VMEM((FOX_HEADS, TQ_FOX, LANE), BF16), pltpu.VMEM((TQ_FOX, D_MODEL), F32),
                        pltpu.VMEM((TQ_FOX, D_MODEL), F32), pltpu.VMEM((FOX_GROUP, TQ_FOX, LANE), F32),
                        pltpu.VMEM((FOX_GROUP, TQ_FOX, LANE), F32), pltpu.VMEM((FOX_GROUP, TQ_FOX, LANE), F32),
                        pltpu.VMEM((FOX_GROUP, TQ_FOX, LANE), F32)],
        compiler_params=_params(),
        name="fox_prompt",
    )(x, mod_all, mod_all, mod_all, g, wqg, gq, wo, kb, vb, fsum, fsum_t.reshape(BATCH, FOX_HEADS, nk, TQ_FOX))


def _fox_q_sample_kernel(x_ref, sh_ref, sc_ref, g_ref, wqg_ref, gq_ref, q_ref, og_ref):
    x = x_ref[...]
    g_, r_, _ = x.shape
    u = _modulate(x, g_ref[...], _get_mod(sh_ref, False), _get_mod(sc_ref, False))
    qn, og = _fox_q(u.reshape(g_ * r_, D_MODEL).astype(BF16), wqg_ref, gq_ref)
    q_ref[...] = qn.reshape(g_, r_, D_MODEL)
    og_ref[...] = og.reshape(g_, r_, D_MODEL)


def _fox_out_sample_kernel(x_ref, gt_ref, oatt_ref, og_ref, wo_ref, o_ref):
    x = x_ref[...]
    g_, r_, _ = x.shape
    n = g_ * r_
    gated = (oatt_ref[...] * jax.nn.sigmoid(og_ref[...])).reshape(n, D_MODEL).astype(BF16)
    o_ref[...] = x + _get_mod(gt_ref, False) * _dot(gated, wo_ref[...]).reshape(g_, r_, D_MODEL)


def _paged_kernel(pt_ref, q_ref, kn_ref, vn_ref, lnt_ref, *refs):
    del pt_ref
    k_pages = refs[:N_PAGES]
    v_pages = refs[N_PAGES:2 * N_PAGES]
    l_pages = refs[2 * N_PAGES:3 * N_PAGES]
    o_ref = refs[3 * N_PAGES]
    kn_scr, vn_scr = refs[3 * N_PAGES + 1:]
    nrow = DEC_SEQ * FOX_HEADS

    hrow = lax.broadcasted_iota(jnp.int32, (FOX_HEADS, D_MODEL), 0)
    hcol = lax.broadcasted_iota(jnp.int32, (FOX_HEADS, D_MODEL), 1)
    head_mask = (hcol >> HD_SHIFT) == hrow
    q = q_ref[0]
    qbd = jnp.concatenate(
        [jnp.where(head_mask, jnp.broadcast_to(q[t:t + 1, :], (FOX_HEADS, D_MODEL)), 0.0) for t in range(DEC_SEQ)],
        axis=0).astype(BF16)

    lnt = lnt_ref[0]
    cn = [lnt[:, 0:1]]
    for t in range(1, DEC_SEQ):
        cn.append(cn[-1] + lnt[:, t:t + 1])
    cn_col = jnp.concatenate(cn, axis=0)

    ri = lax.broadcasted_iota(jnp.int32, (PAGE_SIZE, 2 * LANE), 0)
    ci = lax.broadcasted_iota(jnp.int32, (PAGE_SIZE, 2 * LANE), 1)
    suffix = jnp.where(((ci < PAGE_SIZE) & (ri > ci)) | (ci == PAGE_SIZE), 1.0, 0.0).astype(BF16)

    logits = [None] * N_PAGES
    tot = jnp.zeros((FOX_HEADS, 1), F32)
    for idx in reversed(range(N_PAGES)):
        sums = _split_dot_lhs(l_pages[idx][0], suffix, 2)
        bias16 = sums[:, :PAGE_SIZE] + tot
        tot = tot + sums[:, PAGE_SIZE:PAGE_SIZE + 1]
        bias = jnp.concatenate([bias16] * DEC_SEQ, axis=0) + cn_col
        logits[idx] = _dot(qbd, k_pages[idx][0].astype(BF16)) + bias * LOG2E

    kn_scr[...] = jnp.zeros_like(kn_scr)
    vn_scr[...] = jnp.zeros_like(vn_scr)
    kn_scr[0:DEC_SEQ, :] = kn_ref[0]
    vn_scr[0:DEC_SEQ, :] = vn_ref[0]
    lane = lax.broadcasted_iota(jnp.int32, (nrow, LANE), 1)
    step_of_row = lax.broadcasted_iota(jnp.int32, (nrow, LANE), 0) >> HEAD_SHIFT
    cn_keys = jnp.zeros((nrow, LANE), F32)
    for t in range(DEC_SEQ):
        cn_keys = jnp.where(lane == t, jnp.concatenate([cn[t]] * DEC_SEQ, axis=0), cn_keys)
    s_new = _dot_nt(qbd, kn_scr[...].astype(BF16)) + (cn_col - cn_keys) * LOG2E
    logits.append(jnp.where(lane <= step_of_row, s_new, NEG))

    m = logits[0]
    for s in logits[1:]:
        m = jnp.maximum(m, s)
    m = jnp.broadcast_to(jnp.max(m, axis=-1, keepdims=True), (nrow, LANE))
    l = jnp.zeros((nrow, LANE), F32)
    acc = jnp.zeros((nrow, D_MODEL), F32)
    for idx, s in enumerate(logits):
        p = jnp.exp2(s - m)
        l = l + p
        if idx < N_PAGES:
            acc = acc + _dot_nt(p.astype(BF16), v_pages[idx][0].astype(BF16))
        else:
            acc = acc + _dot(p.astype(BF16), vn_scr[...].astype(BF16))
    l = jnp.broadcast_to(jnp.sum(l, axis=-1, keepdims=True), (nrow, LANE))
    out = acc / jnp.tile(l, (1, D_MODEL // LANE))
    for t in range(DEC_SEQ):
        blk = out[t * FOX_HEADS:(t + 1) * FOX_HEADS, :]
        o_ref[0, t:t + 1, :] = jnp.sum(jnp.where(head_mask, blk, 0.0), axis=0, keepdims=True)


def _paged_attend(q, k_new, v_new, lf_new_t, cache_k_t, cache_v_t, cache_lf_t, page_table):
    def page_spec(shape, idx):
        return pl.BlockSpec((1,) + shape, lambda b, pt: (pt[b, idx], 0, 0))

    seq_spec = pl.BlockSpec((1, DEC_SEQ, D_MODEL), lambda b, pt: (b, 0, 0))
    in_specs = [seq_spec, seq_spec, seq_spec, pl.BlockSpec((1, FOX_HEADS, DEC_SEQ), lambda b, pt: (b, 0, 0))]
    in_specs += [page_spec((D_MODEL, PAGE_SIZE), i) for i in range(N_PAGES)]
    in_specs += [page_spec((D_MODEL, PAGE_SIZE), i) for i in range(N_PAGES)]
    in_specs += [page_spec((FOX_HEADS, PAGE_SIZE), i) for i in range(N_PAGES)]
    grid_spec = pltpu.PrefetchScalarGridSpec(
        num_scalar_prefetch=1,
        grid=(DEC_BATCH,),
        in_specs=in_specs,
        out_specs=seq_spec,
        scratch_shapes=[pltpu.VMEM((PAGE_SIZE, D_MODEL), F32), pltpu.VMEM((PAGE_SIZE, D_MODEL), F32)],
    )
    return pl.pallas_call(
        _paged_kernel,
        grid_spec=grid_spec,
        out_shape=jax.ShapeDtypeStruct((DEC_BATCH, DEC_SEQ, D_MODEL), F32),
        compiler_params=pltpu.CompilerParams(dimension_semantics=("arbitrary",), vmem_limit_bytes=VMEM_LIMIT),
        name="fox_paged",
    )(page_table, q, k_new, v_new, lf_new_t, *([cache_k_t] * N_PAGES), *([cache_v_t] * N_PAGES),
      *([cache_lf_t] * N_PAGES))


def _fox_sample(x, mod_all, g, wqg, gq, wo, k_new, v_new, lf_new, cache_k, cache_v, cache_logf, page_table):
    tok = _Tok(False, 0)
    qn, og = pl.pallas_call(
        _fox_q_sample_kernel,
        grid=tok.grid,
        in_specs=[tok.x(D_MODEL), tok.mod(_mod_col(1, 1, 0)), tok.mod(_mod_col(1, 1, 1)), _resident((1, D_MODEL)),
                  _resident(wqg.shape), _resident(gq.shape)],
        out_specs=[tok.x(D_MODEL), tok.x(D_MODEL)],
        out_shape=[jax.ShapeDtypeStruct(x.shape, F32), jax.ShapeDtypeStruct(x.shape, F32)],
        compiler_params=_params(),
        name="fox_q_sample",
    )(x, mod_all, mod_all, g, wqg, gq)

    n_phys = cache_k.shape[0]
    oatt = _paged_attend(
        qn.transpose(1, 0, 2), k_new.transpose(1, 0, 2), v_new.transpose(1, 0, 2), lf_new.transpose(1, 2, 0),
        cache_k.transpose(0, 2, 3, 1).reshape(n_phys, D_MODEL, PAGE_SIZE),
        cache_v.transpose(0, 2, 3, 1).reshape(n_phys, D_MODEL, PAGE_SIZE),
        cache_logf.transpose(0, 2, 1), page_table)

    return pl.pallas_call(
        _fox_out_sample_kernel,
        grid=tok.grid,
        in_specs=[tok.x(D_MODEL), tok.mod(_mod_col(1, 1, 2)), tok.x(D_MODEL), tok.x(D_MODEL), _resident(wo.shape)],
        out_specs=tok.x(D_MODEL),
        out_shape=jax.ShapeDtypeStruct(x.shape, F32),
        compiler_params=_params(),
        name="fox_out_sample",
    )(x, mod_all, oatt.transpose(1, 0, 2), og, wo)


def kernel(x_prompt, x_sample, state_gla, cache_k, cache_v, cache_logf, page_table, c_prompt, c_sample, w_ada, b_ada, g_norm, w_ffn_up, w_ffn_down, gla_w_in, gla_w_gate2, gla_b_gate, gla_g_out, gla_w_out, w_ada_kv, b_ada_kv, g_kv, w_kvf, b_f, g_k, fox_w_qg, fox_g_q, fox_w_o):
    w_mod = jnp.concatenate([w_ada[0], w_ada[1], w_ada_kv], axis=1).astype(BF16)
    b_mod = jnp.concatenate([b_ada[0], b_ada[1], b_ada_kv], axis=0)[None, :]
    c_all = jnp.concatenate([c_sample, c_prompt, jnp.zeros((MOD_ROWS - DEC_BATCH - BATCH, D_MODEL), F32)], axis=0)
    wu = w_ffn_up.astype(BF16)
    wd = w_ffn_down.astype(BF16)
    w_in = gla_w_in[0]
    wqkvr = w_in[:, :QKVR].astype(BF16)
    wglr = jnp.pad(w_in[:, QKVR:], ((0, 0), (0, LANE - GLA_RANK))).astype(BF16)
    wg2 = jnp.pad(gla_w_gate2[0], ((0, LANE - GLA_RANK), (0, 0))).astype(BF16)
    bg = gla_b_gate[0][None, :]
    gout = gla_g_out[0][None, :]
    wout = gla_w_out[0].astype(BF16)
    wkv = w_kvf[:, :2 * D_MODEL].astype(BF16)
    wf = jnp.pad(w_kvf[:, 2 * D_MODEL:], ((0, 0), (0, LANE - FOX_HEADS))).astype(BF16)
    bf = jnp.pad(b_f, (0, LANE - FOX_HEADS))[None, :]
    gk = jnp.tile(g_k, FOX_HEADS)[None, :]
    wqg = fox_w_qg[0].astype(BF16)
    gq = jnp.tile(fox_g_q[0], FOX_HEADS)[None, :]
    wo = fox_w_o[0].astype(BF16)
    gkv = g_kv[None, :]

    def gn(layer, sub):
        return g_norm[layer, sub][None, :]

    mod_all = _ada(c_all, w_mod, b_mod)

    tok = _Tok(True, TM_FFN)
    h = _ffn(tok, x_prompt, mod_all, 0, 0, gn(0, 0), wu, wd,"ffn_p00")
    s0 = jnp.zeros((BATCH, GLA_HEADS, GLA_DK, GLA_DV), F32)
    h, sg_prompt = _gla_prompt(h, mod_all, gn(0, 1), wqkvr, wglr, wg2, bg, gout, wout, s0)
    h = _ffn(tok, h, mod_all, 0, 2, gn(0, 2), wu, wd,"ffn_p02")
    k_p, v_p, lf_p, kb, vb, fsum, fsum_t = _kv(_Tok(True, TM_KV), h, mod_all, gkv, wkv, wf, bf, gk, "kv_prompt")
    h = _ffn(tok, h, mod_all, 1, 0, gn(1, 0), wu, wd,"ffn_p10")
    h = _fox_prompt(h, mod_all, gn(1, 1), wqg, gq, wo, kb, vb, fsum, fsum_t)
    y_prompt = _ffn(tok, h, mod_all, 1, 2, gn(1, 2), wu, wd,"ffn_p12")

    tok = _Tok(False, 0)
    hs = x_sample.transpose(1, 0, 2)
    hs = _ffn(tok, hs, mod_all, 0, 0, gn(0, 0), wu, wd,"ffn_s00")
    hs, sg_sample = _gla_sample(hs, mod_all, gn(0, 1), wqkvr, wglr, wg2, bg, gout, wout, state_gla[0])
    hs = _ffn(tok, hs, mod_all, 0, 2, gn(0, 2), wu, wd,"ffn_s02")
    k_s, v_s, lf_s = _kv(tok, hs, mod_all, gkv, wkv, wf, bf, gk, "kv_sample")
    hs = _ffn(tok, hs, mod_all, 1, 0, gn(1, 0), wu, wd,"ffn_s10")
    hs = _fox_sample(hs, mod_all, gn(1, 1), wqg, gq, wo, k_s, v_s, lf_s, cache_k, cache_v, cache_logf, page_table)
    hs = _ffn(tok, hs, mod_all, 1, 2, gn(1, 2), wu, wd,"ffn_s12")
    y_sample = hs.transpose(1, 0, 2)

    def heads(t, lead):
        return t.reshape(lead + (FOX_HEADS, FOX_HD))

    return (y_prompt, y_sample, sg_prompt[None],
            heads(k_p, (BATCH, SEQ)), heads(v_p, (BATCH, SEQ)), lf_p,
            sg_sample[None],
            heads(k_s.transpose(1, 0, 2), (DEC_BATCH, DEC_SEQ)), heads(v_s.transpose(1, 0, 2), (DEC_BATCH, DEC_SEQ)),
            lf_s.transpose(1, 0, 2))
```

```python
import functools

import jax
import jax.numpy as jnp
from jax import lax
from jax.experimental import pallas as pl
from jax.experimental.pallas import tpu as pltpu

F32 = jnp.float32
BF16 = jnp.bfloat16

D_MODEL = 1024
BATCH = 4
SEQ = 4096
DEC_BATCH = 128
DEC_SEQ = 4
PAST_LEN = 2048
PAGE_SIZE = 128
N_PAGES = PAST_LEN // PAGE_SIZE
GLA_HEADS = 4
GLA_DK = 128
GLA_DV = 256
GLA_RANK = 16
GLA_TAU = 16.0
FOX_HEADS = 16
FOX_HD = 64
D_FF = 2816
NORM_EPS = 1e-6
HD_SHIFT = FOX_HD.bit_length() - 1
HEAD_SHIFT = FOX_HEADS.bit_length() - 1

DK_ALL = GLA_HEADS * GLA_DK
DV_ALL = GLA_HEADS * GLA_DV
QKVR = 2 * DK_ALL + 2 * DV_ALL
LANE = 128
SUBLANE = 8

NEG = -1e30
LOG2E = 1.4426950408889634
VMEM_LIMIT = 56 * 1024 * 1024

TM_FFN = 512
FF_CHUNK = 1408
TM_GLA = 512
GLA_CHUNK = 128
TM_KV = 512
TQ_FOX = 512
FOX_GROUP = 4
ADA_TN = 1024
SCAN_BS = 8

MOD_ROWS = DEC_BATCH + SUBLANE
PROMPT_ROW_BLOCK = DEC_BATCH // SUBLANE


def _mod_col(layer, sub, kind):
    return layer * 9 + sub * 3 + kind


def _dot(a, b):
    return jnp.dot(a, b, preferred_element_type=F32)


def _dot_nt(a, b):
    return lax.dot_general(a, b, (((1,), (1,)), ((), ())), preferred_element_type=F32)


def _dot_tn(a, b):
    return lax.dot_general(a, b, (((0,), (0,)), ((), ())), preferred_element_type=F32)


def _split_dot(a_bf, x, terms):
    out = None
    r = x
    for _ in range(terms):
        p = r.astype(BF16)
        r = r - p.astype(F32)
        d = _dot(a_bf, p)
        out = d if out is None else out + d
    return out


def _split_dot_lhs(x, b_bf, terms):
    out = None
    r = x
    for _ in range(terms):
        p = r.astype(BF16)
        r = r - p.astype(F32)
        d = _dot(p, b_bf)
        out = d if out is None else out + d
    return out


def _log_sigmoid(x):
    return jnp.minimum(x, 0.0) - jnp.log1p(jnp.exp(-jnp.abs(x)))


def _rms(x):
    return x * lax.rsqrt(jnp.mean(x * x, axis=-1, keepdims=True) + NORM_EPS)


def _modulate(x, g, shift, scale):
    return (_rms(x) * g) * (1.0 + scale) + shift


def _get_mod(ref, prompt):
    if prompt:
        return ref[pl.ds(pl.program_id(0), 1), :]
    return ref[...]


def _lower_tri(n, strict=False):
    r = lax.broadcasted_iota(jnp.int32, (n, n), 0)
    c = lax.broadcasted_iota(jnp.int32, (n, n), 1)
    return (r > c) if strict else (r >= c)


def _head_indicator(transposed):
    if transposed:
        h = lax.broadcasted_iota(jnp.int32, (LANE, D_MODEL), 0)
        c = lax.broadcasted_iota(jnp.int32, (LANE, D_MODEL), 1)
    else:
        c = lax.broadcasted_iota(jnp.int32, (D_MODEL, LANE), 0)
        h = lax.broadcasted_iota(jnp.int32, (D_MODEL, LANE), 1)
    return jnp.where((c >> HD_SHIFT) == h, 1.0, 0.0).astype(BF16)


def _head_inv_rms(x):
    ss = _split_dot_lhs(x * x, _head_indicator(False), 2)
    inv = lax.rsqrt(ss * (1.0 / FOX_HD) + NORM_EPS)
    return _split_dot_lhs(inv, _head_indicator(True), 2)


class _Tok:
    def __init__(self, prompt, tm):
        self.prompt = prompt
        if prompt:
            self.grid = (BATCH, SEQ // tm)
            self.rows = (1, tm)
            self.mod_block = (SUBLANE, D_MODEL)
            self.mod_row = PROMPT_ROW_BLOCK
        else:
            self.grid = (1, 1)
            self.rows = (DEC_SEQ, DEC_BATCH)
            self.mod_block = (DEC_BATCH, D_MODEL)
            self.mod_row = 0

    def x(self, n, col=0):
        if self.prompt:
            return pl.BlockSpec(self.rows + (n,), lambda b, i: (b, i, col))
        return pl.BlockSpec(self.rows + (n,), lambda b, i: (0, 0, col))

    def mod(self, col):
        row = self.mod_row
        return pl.BlockSpec(self.mod_block, lambda b, i: (row, col))

    def shape(self, n):
        return (BATCH, SEQ, n) if self.prompt else (DEC_SEQ, DEC_BATCH, n)


def _resident(shape):
    nd = len(shape)
    return pl.BlockSpec(shape, lambda *_: (0,) * nd, pipeline_mode=pl.Buffered(1))


def _resident_at(shape, lead):
    nd = len(shape) - len(lead)
    block = (None,) * len(lead) + tuple(shape[len(lead):])
    return pl.BlockSpec(block, lambda *_: tuple(lead) + (0,) * nd, pipeline_mode=pl.Buffered(1))


def _params():
    return pltpu.CompilerParams(dimension_semantics=("arbitrary", "arbitrary"), vmem_limit_bytes=VMEM_LIMIT)


def _ada_kernel(c_ref, w_ref, b_ref, o_ref):
    sc = jax.nn.silu(c_ref[...]).astype(BF16)
    o_ref[...] = _dot(sc, w_ref[...].astype(BF16)) + b_ref[...]


def _ada(c_all, w, b, name):
    layers, _, n = w.shape
    nj = n // ADA_TN
    return pl.pallas_call(
        _ada_kernel,
        grid=(layers, nj),
        in_specs=[pl.BlockSpec((MOD_ROWS, D_MODEL), lambda l, j: (0, 0)),
                  pl.BlockSpec((None, D_MODEL, ADA_TN), lambda l, j: (l, 0, j)),
                  pl.BlockSpec((None, 1, ADA_TN), lambda l, j: (l, 0, j))],
        out_specs=pl.BlockSpec((MOD_ROWS, ADA_TN), lambda l, j: (0, l * nj + j)),
        out_shape=jax.ShapeDtypeStruct((MOD_ROWS, layers * n), F32),
        compiler_params=_params(),
        name=name,
    )(c_all, w, b)


def _ffn_kernel(x_ref, sh_ref, sc_ref, gt_ref, g_ref, wu_ref, wd_ref, o_ref, *, prompt):
    x = x_ref[...]
    g_, r_, _ = x.shape
    n = g_ * r_
    u = _modulate(x, g_ref[...], _get_mod(sh_ref, prompt), _get_mod(sc_ref, prompt))
    u = u.reshape(n, D_MODEL).astype(BF16)
    acc = None
    for c in range(D_FF // FF_CHUNK):
        lo = c * FF_CHUNK
        a = _dot(u, wu_ref[:, lo:lo + FF_CHUNK])
        b = _dot(u, wu_ref[:, D_FF + lo:D_FF + lo + FF_CHUNK])
        gated = (jax.nn.silu(a) * b).astype(BF16)
        part = _dot(gated, wd_ref[lo:lo + FF_CHUNK, :])
        acc = part if acc is None else acc + part
    o_ref[...] = x + (0.5 * _get_mod(gt_ref, prompt)) * acc.reshape(g_, r_, D_MODEL)


def _ffn(tok, x, mod_all, layer, sub, g, wu, wd, name):
    which = (layer, sub // 2)
    return pl.pallas_call(
        functools.partial(_ffn_kernel, prompt=tok.prompt),
        grid=tok.grid,
        in_specs=[tok.x(D_MODEL),
                  tok.mod(_mod_col(layer, sub, 0)), tok.mod(_mod_col(layer, sub, 1)), tok.mod(_mod_col(layer, sub, 2)),
                  _resident((1, D_MODEL)), _resident_at(wu.shape, which), _resident_at(wd.shape, which)],
        out_specs=tok.x(D_MODEL),
        out_shape=jax.ShapeDtypeStruct(x.shape, F32),
        compiler_params=_params(),
        name=name,
    )(x, mod_all, mod_all, mod_all, g, wu, wd)


def _gla_in(u, wqkvr_ref, wglr_ref, wg2_ref, bg_ref):
    proj = _dot(u, wqkvr_ref[...])
    glr = _dot(u, wglr_ref[...])
    xg = _dot(glr.astype(BF16), wg2_ref[...]) + bg_ref[...]
    return proj, _log_sigmoid(xg) * (1.0 / GLA_TAU)


def _gla_out(o, r, gout, wout_ref):
    heads = []
    for h in range(GLA_HEADS):
        oh = o[:, h * GLA_DV:(h + 1) * GLA_DV]
        heads.append(_rms(oh) * gout)
    y = (jnp.concatenate(heads, axis=-1) * jax.nn.silu(r)).astype(BF16)
    return _dot(y, wout_ref[...])


def _gla_prompt_kernel(x_ref, sh_ref, sc_ref, gt_ref, g_ref, wqkvr_ref, wglr_ref, wg2_ref, bg_ref, gout_ref,
                       wout_ref, s0_ref, o_ref, sout_ref, proj_scr, la_scr, oscan_scr, st_scr):
    i = pl.program_id(1)
    tm = x_ref.shape[1]
    x = x_ref[0]
    u = _modulate(x, g_ref[...], _get_mod(sh_ref, True), _get_mod(sc_ref, True)).astype(BF16)
    proj, log_a = _gla_in(u, wqkvr_ref, wglr_ref, wg2_ref, bg_ref)
    proj_scr[...] = proj
    la_scr[...] = log_a

    @pl.when(i == 0)
    def _():
        for h in range(GLA_HEADS):
            st_scr[h] = s0_ref[0, h].T

    tri = jnp.where(_lower_tri(GLA_CHUNK), 1.0, 0.0).astype(BF16)
    causal = _lower_tri(GLA_CHUNK)
    qscale = GLA_DK ** -0.5

    states = [st_scr[h] for h in range(GLA_HEADS)]
    for c in range(tm // GLA_CHUNK):
        rows = pl.ds(c * GLA_CHUNK, GLA_CHUNK)
        bcum = _split_dot(tri, la_scr[rows, :], 3)
        blast = bcum[GLA_CHUNK - 1:GLA_CHUNK, :]
        q = proj_scr[rows, 0:DK_ALL] * qscale
        k = proj_scr[rows, DK_ALL:2 * DK_ALL]
        qe = (q * jnp.exp(bcum)).astype(BF16)
        kinv = (k * jnp.exp(-bcum)).astype(BF16)
        kd = (k * jnp.exp(blast - bcum)).astype(BF16)
        elast = jnp.exp(blast)
        for h in range(GLA_HEADS):
            ks = slice(h * GLA_DK, (h + 1) * GLA_DK)
            vlo = 2 * DK_ALL + h * GLA_DV
            v = proj_scr[rows, vlo:vlo + GLA_DV].astype(BF16)
            st = states[h]
            att = jnp.where(causal, _dot_nt(qe[:, ks], kinv[:, ks]), 0.0).astype(BF16)
            oscan_scr[rows, h * GLA_DV:(h + 1) * GLA_DV] = _dot_nt(qe[:, ks], st.astype(BF16)) + _dot(att, v)
            states[h] = st * elast[:, ks] + _dot_tn(v, kd[:, ks])
    for h in range(GLA_HEADS):
        st_scr[h] = states[h]

    r = proj_scr[:, 2 * DK_ALL + DV_ALL:QKVR]
    mix = _gla_out(oscan_scr[...], r, gout_ref[...], wout_ref)
    o_ref[0] = x + _get_mod(gt_ref, True) * mix

    @pl.when(i == pl.num_programs(1) - 1)
    def _():
        for h in range(GLA_HEADS):
            sout_ref[0, h] = st_scr[h].T


def _gla_prompt(x, mod_all, g, wqkvr, wglr, wg2, bg, gout, wout, s0):
    tok = _Tok(True, TM_GLA)
    state_spec = pl.BlockSpec((1, GLA_HEADS, GLA_DK, GLA_DV), lambda b, i: (b, 0, 0, 0))
    return pl.pallas_call(
        _gla_prompt_kernel,
        grid=tok.grid,
        in_specs=[tok.x(D_MODEL), tok.mod(_mod_col(0, 1, 0)), tok.mod(_mod_col(0, 1, 1)), tok.mod(_mod_col(0, 1, 2)),
                  _resident((1, D_MODEL)), _resident(wqkvr.shape), _resident(wglr.shape), _resident(wg2.shape),
                  _resident(bg.shape), _resident(gout.shape), _resident(wout.shape), state_spec],
        out_specs=[tok.x(D_MODEL), state_spec],
        out_shape=[jax.ShapeDtypeStruct(x.shape, F32), jax.ShapeDtypeStruct(s0.shape, F32)],
        scratch_shapes=[pltpu.VMEM((TM_GLA, QKVR), F32), pltpu.VMEM((TM_GLA, DK_ALL), F32),
                        pltpu.VMEM((TM_GLA, DV_ALL), F32), pltpu.VMEM((GLA_HEADS, GLA_DV, GLA_DK), F32)],
        compiler_params=_params(),
        name="gla_prompt",
    )(x, mod_all, mod_all, mod_all, g, wqkvr, wglr, wg2, bg, gout, wout, s0)


def _gla_in_sample_kernel(x_ref, sh_ref, sc_ref, g_ref, wqkvr_ref, wglr_ref, wg2_ref, bg_ref, proj_ref, la_ref):
    x = x_ref[...]
    g_, r_, _ = x.shape
    u = _modulate(x, g_ref[...], _get_mod(sh_ref, False), _get_mod(sc_ref, False))
    u = u.reshape(g_ * r_, D_MODEL).astype(BF16)
    proj, log_a = _gla_in(u, wqkvr_ref, wglr_ref, wg2_ref, bg_ref)
    proj_ref[...] = proj.reshape(g_, r_, QKVR)
    la_ref[...] = log_a.reshape(g_, r_, DK_ALL)


def _gla_scan_sample_kernel(proj_ref, la_ref, s0_ref, o_ref, sout_ref, xt_scr):
    qscale = GLA_DK ** -0.5
    n_kind = DEC_SEQ * GLA_HEADS
    xt_scr[...] = jnp.zeros_like(xt_scr)

    def seq(j, carry):
        for t in range(DEC_SEQ):
            a_t = jnp.exp(la_ref[t, pl.ds(j, 1), :])
            q_t = proj_ref[t, pl.ds(j, 1), 0:DK_ALL] * qscale
            k_t = proj_ref[t, pl.ds(j, 1), DK_ALL:2 * DK_ALL]
            for h in range(GLA_HEADS):
                ks = slice(h * GLA_DK, (h + 1) * GLA_DK)
                row = h * DEC_SEQ + t
                xt_scr[row:row + 1, :] = a_t[:, ks]
                xt_scr[n_kind + row:n_kind + row + 1, :] = q_t[:, ks]
                xt_scr[2 * n_kind + row:2 * n_kind + row + 1, :] = k_t[:, ks]
        xt = xt_scr[...].T
        for h in range(GLA_HEADS):
            s = s0_ref[j, h]
            for t in range(DEC_SEQ):
                row = h * DEC_SEQ + t
                a_c = xt[:, row:row + 1]
                q_c = xt[:, n_kind + row:n_kind + row + 1]
                k_c = xt[:, 2 * n_kind + row:2 * n_kind + row + 1]
                vlo = 2 * DK_ALL + h * GLA_DV
                v_t = proj_ref[t, pl.ds(j, 1), vlo:vlo + GLA_DV]
                s = a_c * s + k_c * v_t
                o_ref[t, pl.ds(j, 1), h * GLA_DV:(h + 1) * GLA_DV] = jnp.sum(q_c * s, axis=0, keepdims=True)
            sout_ref[j, h] = s
        return carry

    lax.fori_loop(0, SCAN_BS, seq, 0)


def _gla_out_sample_kernel(x_ref, gt_ref, oscan_ref, r_ref, gout_ref, wout_ref, o_ref):
    x = x_ref[...]
    g_, r_, _ = x.shape
    n = g_ * r_
    mix = _gla_out(oscan_ref[...].reshape(n, DV_ALL), r_ref[...].reshape(n, DV_ALL), gout_ref[...], wout_ref)
    o_ref[...] = x + _get_mod(gt_ref, False) * mix.reshape(g_, r_, D_MODEL)


def _gla_sample(x, mod_all, g, wqkvr, wglr, wg2, bg, gout, wout, s0):
    tok = _Tok(False, 0)
    proj, log_a = pl.pallas_call(
        _gla_in_sample_kernel,
        grid=tok.grid,
        in_specs=[tok.x(D_MODEL), tok.mod(_mod_col(0, 1, 0)), tok.mod(_mod_col(0, 1, 1)),
                  _resident((1, D_MODEL)), _resident(wqkvr.shape), _resident(wglr.shape), _resident(wg2.shape),
                  _resident(bg.shape)],
        out_specs=[tok.x(QKVR), tok.x(DK_ALL)],
        out_shape=[jax.ShapeDtypeStruct(tok.shape(QKVR), F32), jax.ShapeDtypeStruct(tok.shape(DK_ALL), F32)],
        compiler_params=_params(),
        name="gla_in_sample",
    )(x, mod_all, mod_all, g, wqkvr, wglr, wg2, bg)

    state_spec = pl.BlockSpec((SCAN_BS, GLA_HEADS, GLA_DK, GLA_DV), lambda j: (j, 0, 0, 0))
    oscan, s_out = pl.pallas_call(
        _gla_scan_sample_kernel,
        grid=(DEC_BATCH // SCAN_BS,),
        in_specs=[pl.BlockSpec((DEC_SEQ, SCAN_BS, QKVR), lambda j: (0, j, 0)),
                  pl.BlockSpec((DEC_SEQ, SCAN_BS, DK_ALL), lambda j: (0, j, 0)),
                  state_spec],
        out_specs=[pl.BlockSpec((DEC_SEQ, SCAN_BS, DV_ALL), lambda j: (0, j, 0)), state_spec],
        out_shape=[jax.ShapeDtypeStruct(tok.shape(DV_ALL), F32), jax.ShapeDtypeStruct(s0.shape, F32)],
        scratch_shapes=[pltpu.VMEM((LANE, LANE), F32)],
        compiler_params=pltpu.CompilerParams(dimension_semantics=("arbitrary",), vmem_limit_bytes=VMEM_LIMIT),
        name="gla_scan_sample",
    )(proj, log_a, s0)

    h = pl.pallas_call(
        _gla_out_sample_kernel,
        grid=tok.grid,
        in_specs=[tok.x(D_MODEL), tok.mod(_mod_col(0, 1, 2)), tok.x(DV_ALL),
                  tok.x(DV_ALL, col=(2 * DK_ALL + DV_ALL) // DV_ALL), _resident(gout.shape), _resident(wout.shape)],
        out_specs=tok.x(D_MODEL),
        out_shape=jax.ShapeDtypeStruct(x.shape, F32),
        compiler_params=_params(),
        name="gla_out_sample",
    )(x, mod_all, oscan, proj, gout, wout)
    return h, s_out


def _kv_kernel(x_ref, sh_ref, sc_ref, g_ref, wkv_ref, wf_ref, bf_ref, gk_ref, *refs, prompt):
    if prompt:
        k_ref, v_ref, lf_ref, kb_ref, vb_ref, f_ref, ft_ref, carry_scr = refs
    else:
        k_ref, v_ref, lf_ref = refs
    x = x_ref[...]
    g_, r_, _ = x.shape
    n = g_ * r_
    u = _modulate(x, g_ref[...], _get_mod(sh_ref, prompt), _get_mod(sc_ref, prompt))
    u = u.reshape(n, D_MODEL).astype(BF16)
    kv = _dot(u, wkv_ref[...])
    k = kv[:, :D_MODEL]
    v = kv[:, D_MODEL:]
    kn = k * _head_inv_rms(k) * gk_ref[...]
    lf = _log_sigmoid(_dot(u, wf_ref[...]) + bf_ref[...])
    k_ref[...] = kn.reshape(g_, r_, D_MODEL)
    v_ref[...] = v.reshape(g_, r_, D_MODEL)
    lf_ref[...] = lf[:, :FOX_HEADS].reshape(g_, r_, FOX_HEADS)
    if prompt:
        kb_ref[...] = kn.astype(BF16).reshape(g_, r_, D_MODEL)
        vb_ref[...] = v.astype(BF16).reshape(g_, r_, D_MODEL)

        @pl.when(pl.program_id(1) == 0)
        def _():
            carry_scr[...] = jnp.zeros_like(carry_scr)

        tri = jnp.where(_lower_tri(n), 1.0, 0.0).astype(BF16)
        fsum = _split_dot(tri, lf, 3) + carry_scr[...]
        carry_scr[...] = fsum[n - 1:n, :]
        f_ref[...] = fsum[:, :FOX_HEADS].reshape(g_, r_, FOX_HEADS)
        ft_ref[0] = fsum.T[:FOX_HEADS, :]


def _kv(tok, x, mod_kv, g, wkv, wf, bf, gk, name):
    prompt = tok.prompt
    out_specs = [tok.x(D_MODEL), tok.x(D_MODEL), tok.x(FOX_HEADS)]
    out_shape = [jax.ShapeDtypeStruct(tok.shape(D_MODEL), F32), jax.ShapeDtypeStruct(tok.shape(D_MODEL), F32),
                 jax.ShapeDtypeStruct(tok.shape(FOX_HEADS), F32)]
    scratch = []
    if prompt:
        out_specs += [tok.x(D_MODEL), tok.x(D_MODEL), tok.x(FOX_HEADS),
                      pl.BlockSpec((1, FOX_HEADS, TM_KV), lambda b, i: (b, 0, i))]
        out_shape += [jax.ShapeDtypeStruct(tok.shape(D_MODEL), BF16), jax.ShapeDtypeStruct(tok.shape(D_MODEL), BF16),
                      jax.ShapeDtypeStruct(tok.shape(FOX_HEADS), F32),
                      jax.ShapeDtypeStruct((BATCH, FOX_HEADS, SEQ), F32)]
        scratch = [pltpu.VMEM((1, LANE), F32)]
    return pl.pallas_call(
        functools.partial(_kv_kernel, prompt=prompt),
        grid=tok.grid,
        in_specs=[tok.x(D_MODEL), tok.mod(0), tok.mod(1), _resident((1, D_MODEL)), _resident(wkv.shape),
                  _resident(wf.shape), _resident(bf.shape), _resident(gk.shape)],
        out_specs=out_specs,
        out_shape=out_shape,
        scratch_shapes=scratch,
        compiler_params=_params(),
        name=name,
    )(x, mod_kv, mod_kv, g, wkv, wf, bf, gk)


def _fox_q(u, wqg_ref, gq_ref):
    qg = _dot(u, wqg_ref[...])
    q = qg[:, :D_MODEL]
    og = qg[:, D_MODEL:]
    qn = q * _head_inv_rms(q) * gq_ref[...] * (FOX_HD ** -0.5 * LOG2E)
    return qn, og


def _softmax_step(t, row_bias, m, l, acc, v):
    m_cur = jnp.max(t, axis=-1, keepdims=True)
    if row_bias is not None:
        m_cur = m_cur + row_bias
    m_new = jnp.maximum(m, m_cur)
    alpha = jnp.exp2(m - m_new)
    shift = -m_new if row_bias is None else row_bias - m_new
    p = jnp.exp2(t + jnp.tile(shift, (1, t.shape[1] // LANE)))
    l_new = alpha * l + jnp.sum(p, axis=-1, keepdims=True)
    pv = _dot(p.astype(BF16), v)
    acc_new = jnp.tile(alpha, (1, pv.shape[1] // LANE)) * acc + pv
    return m_new, l_new, acc_new


def _fox_prompt_kernel(x_ref, sh_ref, sc_ref, gt_ref, g_ref, wqg_ref, gq_ref, wo_ref, kb_ref, vb_ref, f_ref, ft_ref,
                       o_ref, qm_scr, og_scr, oatt_scr, m_scr, l_scr, acc_scr, fq_scr):
    i = pl.program_id(1)
    tq = x_ref.shape[1]
    x = x_ref[0]
    u = _modulate(x, g_ref[...], _get_mod(sh_ref, True), _get_mod(sc_ref, True)).astype(BF16)
    qn, og = _fox_q(u, wqg_ref, gq_ref)
    og_scr[...] = og
    first = lax.broadcasted_iota(jnp.int32, (tq, LANE), 1) < FOX_HD

    for pair in range(FOX_HEADS // 2):
        q2 = qn[:, pair * LANE:(pair + 1) * LANE]
        qm_scr[2 * pair] = jnp.where(first, q2, 0.0).astype(BF16)
        qm_scr[2 * pair + 1] = jnp.where(first, 0.0, q2).astype(BF16)

    fq_all = f_ref[0]
    causal = _lower_tri(tq)

    for grp in range(FOX_HEADS // FOX_GROUP):
        m_scr[...] = jnp.full_like(m_scr, NEG)
        l_scr[...] = jnp.zeros_like(l_scr)
        acc_scr[...] = jnp.zeros_like(acc_scr)
        for idx in range(FOX_GROUP):
            head = grp * FOX_GROUP + idx
            fq_scr[idx] = jnp.broadcast_to(fq_all[:, head:head + 1], (tq, LANE)) * LOG2E

        def step(j, diagonal, grp=grp):
            rows = pl.ds(pl.multiple_of(j * tq, tq), tq)
            for idx in range(FOX_GROUP):
                head = grp * FOX_GROUP + idx
                cols = slice((head // 2) * LANE, (head // 2 + 1) * LANE)
                fk = ft_ref[0, head, pl.ds(j, 1), :] * LOG2E
                t = _dot_nt(qm_scr[head], kb_ref[0, rows, cols]) - fk
                if diagonal:
                    t = jnp.where(causal, t, NEG)
                m, l, acc = _softmax_step(t, fq_scr[idx], m_scr[idx], l_scr[idx], acc_scr[idx],
                                          vb_ref[0, rows, cols])
                m_scr[idx] = m
                l_scr[idx] = l
                acc_scr[idx] = acc

        def off_diagonal(j, carry):
            step(j, False)
            return carry

        lax.fori_loop(0, i, off_diagonal, 0)
        step(i, True)
        for pp in range(FOX_GROUP // 2):
            pair = grp * (FOX_GROUP // 2) + pp
            o2 = jnp.where(first, acc_scr[2 * pp] / l_scr[2 * pp], acc_scr[2 * pp + 1] / l_scr[2 * pp + 1])
            oatt_scr[:, pair * LANE:(pair + 1) * LANE] = o2

    gated = (oatt_scr[...] * jax.nn.sigmoid(og_scr[...])).astype(BF16)
    o_ref[0] = x + _get_mod(gt_ref, True) * _dot(gated, wo_ref[...])


def _fox_prompt(x, mod_all, g, wqg, gq, wo, kb, vb, fsum, fsum_t):
    tok = _Tok(True, TQ_FOX)
    nk = SEQ // TQ_FOX
    seq_spec = pl.BlockSpec((1, SEQ, D_MODEL), lambda b, i: (b, 0, 0), pipeline_mode=pl.Buffered(1))
    return pl.pallas_call(
        _fox_prompt_kernel,
        grid=tok.grid,
        in_specs=[tok.x(D_MODEL), tok.mod(_mod_col(1, 1, 0)), tok.mod(_mod_col(1, 1, 1)), tok.mod(_mod_col(1, 1, 2)),
                  _resident((1, D_MODEL)), _resident(wqg.shape), _resident(gq.shape), _resident(wo.shape),
                  seq_spec, seq_spec, tok.x(FOX_HEADS),
                  pl.BlockSpec((1, FOX_HEADS, nk, TQ_FOX), lambda b, i: (b, 0, 0, 0))],
        out_specs=tok.x(D_MODEL),
        out_shape=jax.ShapeDtypeStruct(x.shape, F32),
        scratch_shapes=[pltpu.VMEM((FOX_HEADS, TQ_FOX, LANE), BF16), pltpu.VMEM((TQ_FOX, D_MODEL), F32),
                        pltpu.VMEM((TQ_FOX, D_MODEL), F32), pltpu.VMEM((FOX_GROUP, TQ_FOX, LANE), F32),
                        pltpu.VMEM((FOX_GROUP, TQ_FOX, LANE), F32), pltpu.VMEM((FOX_GROUP, TQ_FOX, LANE), F32),
                        pltpu.VMEM((FOX_GROUP, TQ_FOX, LANE), F32)],
        compiler_params=_params(),
        name="fox_prompt",
    )(x, mod_all, mod_all, mod_all, g, wqg, gq, wo, kb, vb, fsum, fsum_t.reshape(BATCH, FOX_HEADS, nk, TQ_FOX))


def _fox_q_sample_kernel(x_ref, sh_ref, sc_ref, g_ref, wqg_ref, gq_ref, q_ref, og_ref):
    x = x_ref[...]
    g_, r_, _ = x.shape
    u = _modulate(x, g_ref[...], _get_mod(sh_ref, False), _get_mod(sc_ref, False))
    qn, og = _fox_q(u.reshape(g_ * r_, D_MODEL).astype(BF16), wqg_ref, gq_ref)
    q_ref[...] = qn.reshape(g_, r_, D_MODEL)
    og_ref[...] = og.reshape(g_, r_, D_MODEL)


def _fox_out_sample_kernel(x_ref, gt_ref, oatt_ref, og_ref, wo_ref, o_ref):
    x = x_ref[...]
    g_, r_, _ = x.shape
    n = g_ * r_
    gated = (oatt_ref[...] * jax.nn.sigmoid(og_ref[...])).reshape(n, D_MODEL).astype(BF16)
    o_ref[...] = x + _get_mod(gt_ref, False) * _dot(gated, wo_ref[...]).reshape(g_, r_, D_MODEL)


def _paged_kernel(pt_ref, q_ref, kn_ref, vn_ref, lnt_ref, *refs):
    del pt_ref
    k_pages = refs[:N_PAGES]
    v_pages = refs[N_PAGES:2 * N_PAGES]
    l_pages = refs[2 * N_PAGES:3 * N_PAGES]
    o_ref = refs[3 * N_PAGES]
    kn_scr, vn_scr = refs[3 * N_PAGES + 1:]
    nrow = DEC_SEQ * FOX_HEADS

    hrow = lax.broadcasted_iota(jnp.int32, (FOX_HEADS, D_MODEL), 0)
    hcol = lax.broadcasted_iota(jnp.int32, (FOX_HEADS, D_MODEL), 1)
    head_mask = (hcol >> HD_SHIFT) == hrow
    q = q_ref[0]
    qbd = jnp.concatenate(
        [jnp.where(head_mask, jnp.broadcast_to(q[t:t + 1, :], (FOX_HEADS, D_MODEL)), 0.0) for t in range(DEC_SEQ)],
        axis=0).astype(BF16)

    lnt = lnt_ref[0]
    cn = [lnt[:, 0:1]]
    for t in range(1, DEC_SEQ):
        cn.append(cn[-1] + lnt[:, t:t + 1])
    cn_col = jnp.concatenate(cn, axis=0)

    ri = lax.broadcasted_iota(jnp.int32, (PAGE_SIZE, 2 * LANE), 0)
    ci = lax.broadcasted_iota(jnp.int32, (PAGE_SIZE, 2 * LANE), 1)
    suffix = jnp.where(((ci < PAGE_SIZE) & (ri > ci)) | (ci == PAGE_SIZE), 1.0, 0.0).astype(BF16)

    logits = [None] * N_PAGES
    tot = jnp.zeros((FOX_HEADS, 1), F32)
    for idx in reversed(range(N_PAGES)):
        sums = _split_dot_lhs(l_pages[idx][0], suffix, 2)
        bias16 = sums[:, :PAGE_SIZE] + tot
        tot = tot + sums[:, PAGE_SIZE:PAGE_SIZE + 1]
        bias = jnp.concatenate([bias16] * DEC_SEQ, axis=0) + cn_col
        logits[idx] = _dot(qbd, k_pages[idx][0].astype(BF16)) + bias * LOG2E

    kn_scr[...] = jnp.zeros_like(kn_scr)
    vn_scr[...] = jnp.zeros_like(vn_scr)
    kn_scr[0:DEC_SEQ, :] = kn_ref[0]
    vn_scr[0:DEC_SEQ, :] = vn_ref[0]
    lane = lax.broadcasted_iota(jnp.int32, (nrow, LANE), 1)
    step_of_row = lax.broadcasted_iota(jnp.int32, (nrow, LANE), 0) >> HEAD_SHIFT
    cn_keys = jnp.zeros((nrow, LANE), F32)
    for t in range(DEC_SEQ):
        cn_keys = jnp.where(lane == t, jnp.concatenate([cn[t]] * DEC_SEQ, axis=0), cn_keys)
    s_new = _dot_nt(qbd, kn_scr[...].astype(BF16)) + (cn_col - cn_keys) * LOG2E
    logits.append(jnp.where(lane <= step_of_row, s_new, NEG))

    m = logits[0]
    for s in logits[1:]:
        m = jnp.maximum(m, s)
    m = jnp.broadcast_to(jnp.max(m, axis=-1, keepdims=True), (nrow, LANE))
    l = jnp.zeros((nrow, LANE), F32)
    acc = jnp.zeros((nrow, D_MODEL), F32)
    for idx, s in enumerate(logits):
        p = jnp.exp2(s - m)
        l = l + p
        if idx < N_PAGES:
            acc = acc + _dot_nt(p.astype(BF16), v_pages[idx][0].astype(BF16))
        else:
            acc = acc + _dot(p.astype(BF16), vn_scr[...].astype(BF16))
    l = jnp.broadcast_to(jnp.sum(l, axis=-1, keepdims=True), (nrow, LANE))
    out = acc / jnp.tile(l, (1, D_MODEL // LANE))
    for t in range(DEC_SEQ):
        blk = out[t * FOX_HEADS:(t + 1) * FOX_HEADS, :]
        o_ref[0, t:t + 1, :] = jnp.sum(jnp.where(head_mask, blk, 0.0), axis=0, keepdims=True)


def _paged_attend(q, k_new, v_new, lf_new_t, cache_k_t, cache_v_t, cache_lf_t, page_table):
    def page_spec(shape, idx):
        return pl.BlockSpec((1,) + shape, lambda b, pt: (pt[b, idx], 0, 0))

    seq_spec = pl.BlockSpec((1, DEC_SEQ, D_MODEL), lambda b, pt: (b, 0, 0))
    in_specs = [seq_spec, seq_spec, seq_spec, pl.BlockSpec((1, FOX_HEADS, DEC_SEQ), lambda b, pt: (b, 0, 0))]
    in_specs += [page_spec((D_MODEL, PAGE_SIZE), i) for i in range(N_PAGES)]
    in_specs += [page_spec((D_MODEL, PAGE_SIZE), i) for i in range(N_PAGES)]
    in_specs += [page_spec((FOX_HEADS, PAGE_SIZE), i) for i in range(N_PAGES)]
    grid_spec = pltpu.PrefetchScalarGridSpec(
        num_scalar_prefetch=1,
        grid=(DEC_BATCH,),
        in_specs=in_specs,
        out_specs=seq_spec,
        scratch_shapes=[pltpu.VMEM((PAGE_SIZE, D_MODEL), F32), pltpu.VMEM((PAGE_SIZE, D_MODEL), F32)],
    )
    return pl.pallas_call(
        _paged_kernel,
        grid_spec=grid_spec,
        out_shape=jax.ShapeDtypeStruct((DEC_BATCH, DEC_SEQ, D_MODEL), F32),
        compiler_params=pltpu.CompilerParams(dimension_semantics=("arbitrary",), vmem_limit_bytes=VMEM_LIMIT),
        name="fox_paged",
    )(page_table, q, k_new, v_new, lf_new_t, *([cache_k_t] * N_PAGES), *([cache_v_t] * N_PAGES),
      *([cache_lf_t] * N_PAGES))


def _fox_sample(x, mod_all, g, wqg, gq, wo, k_new, v_new, lf_new, cache_k, cache_v, cache_logf, page_table):
    tok = _Tok(False, 0)
    qn, og = pl.pallas_call(
        _fox_q_sample_kernel,
        grid=tok.grid,
        in_specs=[tok.x(D_MODEL), tok.mod(_mod_col(1, 1, 0)), tok.mod(_mod_col(1, 1, 1)), _resident((1, D_MODEL)),
                  _resident(wqg.shape), _resident(gq.shape)],
        out_specs=[tok.x(D_MODEL), tok.x(D_MODEL)],
        out_shape=[jax.ShapeDtypeStruct(x.shape, F32), jax.ShapeDtypeStruct(x.shape, F32)],
        compiler_params=_params(),
        name="fox_q_sample",
    )(x, mod_all, mod_all, g, wqg, gq)

    n_phys = cache_k.shape[0]
    oatt = _paged_attend(
        qn.transpose(1, 0, 2), k_new.transpose(1, 0, 2), v_new.transpose(1, 0, 2), lf_new.transpose(1, 2, 0),
        cache_k.transpose(0, 2, 3, 1).reshape(n_phys, D_MODEL, PAGE_SIZE),
        cache_v.transpose(0, 2, 3, 1).reshape(n_phys, D_MODEL, PAGE_SIZE),
        cache_logf.transpose(0, 2, 1), page_table)

    return pl.pallas_call(
        _fox_out_sample_kernel,
        grid=tok.grid,
        in_specs=[tok.x(D_MODEL), tok.mod(_mod_col(1, 1, 2)), tok.x(D_MODEL), tok.x(D_MODEL), _resident(wo.shape)],
        out_specs=tok.x(D_MODEL),
        out_shape=jax.ShapeDtypeStruct(x.shape, F32),
        compiler_params=_params(),
        name="fox_out_sample",
    )(x, mod_all, oatt.transpose(1, 0, 2), og, wo)


def kernel(x_prompt, x_sample, state_gla, cache_k, cache_v, cache_logf, page_table, c_prompt, c_sample, w_ada, b_ada, g_norm, w_ffn_up, w_ffn_down, gla_w_in, gla_w_gate2, gla_b_gate, gla_g_out, gla_w_out, w_ada_kv, b_ada_kv, g_kv, w_kvf, b_f, g_k, fox_w_qg, fox_g_q, fox_w_o):
    c_all = jnp.concatenate([c_sample, c_prompt, jnp.zeros((MOD_ROWS - DEC_BATCH - BATCH, D_MODEL), F32)], axis=0)
    wu = w_ffn_up.astype(BF16)
    wd = w_ffn_down.astype(BF16)
    w_in = gla_w_in[0]
    wqkvr = w_in[:, :QKVR].astype(BF16)
    wglr = jnp.pad(w_in[:, QKVR:], ((0, 0), (0, LANE - GLA_RANK))).astype(BF16)
    wg2 = jnp.pad(gla_w_gate2[0], ((0, LANE - GLA_RANK), (0, 0))).astype(BF16)
    bg = gla_b_gate[0][None, :]
    gout = gla_g_out[0][None, :]
    wout = gla_w_out[0].astype(BF16)
    wkv = w_kvf[:, :2 * D_MODEL].astype(BF16)
    wf = jnp.pad(w_kvf[:, 2 * D_MODEL:], ((0, 0), (0, LANE - FOX_HEADS))).astype(BF16)
    bf = jnp.pad(b_f, (0, LANE - FOX_HEADS))[None, :]
    gk = jnp.tile(g_k, FOX_HEADS)[None, :]
    wqg = fox_w_qg[0].astype(BF16)
    gq = jnp.tile(fox_g_q[0], FOX_HEADS)[None, :]
    wo = fox_w_o[0].astype(BF16)
    gkv = g_kv[None, :]

    def gn(layer, sub):
        return g_norm[layer, sub][None, :]

    mod_all = _ada(c_all, w_ada, b_ada[:, None, :], "ada_mod")
    mod_kv = _ada(c_all, w_ada_kv[None], b_ada_kv[None, None, :], "ada_mod_kv")

    tok = _Tok(True, TM_FFN)
    h = _ffn(tok, x_prompt, mod_all, 0, 0, gn(0, 0), wu, wd,"ffn_p00")
    s0 = jnp.zeros((BATCH, GLA_HEADS, GLA_DK, GLA_DV), F32)
    h, sg_prompt = _gla_prompt(h, mod_all, gn(0, 1), wqkvr, wglr, wg2, bg, gout, wout, s0)
    h = _ffn(tok, h, mod_all, 0, 2, gn(0, 2), wu, wd,"ffn_p02")
    k_p, v_p, lf_p, kb, vb, fsum, fsum_t = _kv(_Tok(True, TM_KV), h, mod_kv, gkv, wkv, wf, bf, gk, "kv_prompt")
    h = _ffn(tok, h, mod_all, 1, 0, gn(1, 0), wu, wd,"ffn_p10")
    h = _fox_prompt(h, mod_all, gn(1, 1), wqg, gq, wo, kb, vb, fsum, fsum_t)
    y_prompt = _ffn(tok, h, mod_all, 1, 2, gn(1, 2), wu, wd,"ffn_p12")

    tok = _Tok(False, 0)
    hs = x_sample.transpose(1, 0, 2)
    hs = _ffn(tok, hs, mod_all, 0, 0, gn(0, 0), wu, wd,"ffn_s00")
    hs, sg_sample = _gla_sample(hs, mod_all, gn(0, 1), wqkvr, wglr, wg2, bg, gout, wout, state_gla[0])
    hs = _ffn(tok, hs, mod_all, 0, 2, gn(0, 2), wu, wd,"ffn_s02")
    k_s, v_s, lf_s = _kv(tok, hs, mod_kv, gkv, wkv, wf, bf, gk, "kv_sample")
    hs = _ffn(tok, hs, mod_all, 1, 0, gn(1, 0), wu, wd,"ffn_s10")
    hs = _fox_sample(hs, mod_all, gn(1, 1), wqg, gq, wo, k_s, v_s, lf_s, cache_k, cache_v, cache_logf, page_table)
    hs = _ffn(tok, hs, mod_all, 1, 2, gn(1, 2), wu, wd,"ffn_s12")
    y_sample = hs.transpose(1, 0, 2)

    def heads(t, lead):
        return t.reshape(lead + (FOX_HEADS, FOX_HD))

    return (y_prompt, y_sample, sg_prompt[None],
            heads(k_p, (BATCH, SEQ)), heads(v_p, (BATCH, SEQ)), lf_p,
            sg_sample[None],
            heads(k_s.transpose(1, 0, 2), (DEC_BATCH, DEC_SEQ)), heads(v_s.transpose(1, 0, 2), (DEC_BATCH, DEC_SEQ)),
            lf_s.transpose(1, 0, 2))
```

```python
import functools

import jax
import jax.numpy as jnp
from jax import lax
from jax.experimental import pallas as pl
from jax.experimental.pallas import tpu as pltpu

F32 = jnp.float32
BF16 = jnp.bfloat16

D_MODEL = 1024
BATCH = 4
SEQ = 4096
DEC_BATCH = 128
DEC_SEQ = 4
PAST_LEN = 2048
PAGE_SIZE = 128
N_PAGES = PAST_LEN // PAGE_SIZE
GLA_HEADS = 4
GLA_DK = 128
GLA_DV = 256
GLA_RANK = 16
GLA_TAU = 16.0
FOX_HEADS = 16
FOX_HD = 64
D_FF = 2816
NORM_EPS = 1e-6
HD_SHIFT = FOX_HD.bit_length() - 1
HEAD_SHIFT = FOX_HEADS.bit_length() - 1

DK_ALL = GLA_HEADS * GLA_DK
DV_ALL = GLA_HEADS * GLA_DV
QKVR = 2 * DK_ALL + 2 * DV_ALL
LANE = 128
SUBLANE = 8

NEG = -1e30
LOG2E = 1.4426950408889634
VMEM_LIMIT = 56 * 1024 * 1024

TM_FFN = 512
FF_CHUNK = 1408
TM_GLA = 512
GLA_CHUNK = 128
TM_KV = 512
TQ_FOX = 512
FOX_GROUP = 4
ADA_TN = 1024
SCAN_BS = 8

MOD_ROWS = DEC_BATCH + SUBLANE
PROMPT_ROW_BLOCK = DEC_BATCH // SUBLANE


def _mod_col(layer, sub, kind):
    return layer * 9 + sub * 3 + kind


def _dot(a, b):
    return jnp.dot(a, b, preferred_element_type=F32)


def _dot_nt(a, b):
    return lax.dot_general(a, b, (((1,), (1,)), ((), ())), preferred_element_type=F32)


def _dot_tn(a, b):
    return lax.dot_general(a, b, (((0,), (0,)), ((), ())), preferred_element_type=F32)


def _split_dot(a_bf, x, terms):
    out = None
    r = x
    for _ in range(terms):
        p = r.astype(BF16)
        r = r - p.astype(F32)
        d = _dot(a_bf, p)
        out = d if out is None else out + d
    return out


def _split_dot_lhs(x, b_bf, terms):
    out = None
    r = x
    for _ in range(terms):
        p = r.astype(BF16)
        r = r - p.astype(F32)
        d = _dot(p, b_bf)
        out = d if out is None else out + d
    return out


def _log_sigmoid(x):
    return jnp.minimum(x, 0.0) - jnp.log1p(jnp.exp(-jnp.abs(x)))


def _rms(x):
    return x * lax.rsqrt(jnp.mean(x * x, axis=-1, keepdims=True) + NORM_EPS)


def _modulate(x, g, shift, scale):
    return (_rms(x) * g) * (1.0 + scale) + shift


def _get_mod(ref, prompt):
    if prompt:
        return ref[pl.ds(pl.program_id(0), 1), :]
    return ref[...]


def _lower_tri(n, strict=False):
    r = lax.broadcasted_iota(jnp.int32, (n, n), 0)
    c = lax.broadcasted_iota(jnp.int32, (n, n), 1)
    return (r > c) if strict else (r >= c)


def _head_indicator(transposed):
    if transposed:
        h = lax.broadcasted_iota(jnp.int32, (LANE, D_MODEL), 0)
        c = lax.broadcasted_iota(jnp.int32, (LANE, D_MODEL), 1)
    else:
        c = lax.broadcasted_iota(jnp.int32, (D_MODEL, LANE), 0)
        h = lax.broadcasted_iota(jnp.int32, (D_MODEL, LANE), 1)
    return jnp.where((c >> HD_SHIFT) == h, 1.0, 0.0).astype(BF16)


def _head_inv_rms(x):
    ss = _split_dot_lhs(x * x, _head_indicator(False), 2)
    inv = lax.rsqrt(ss * (1.0 / FOX_HD) + NORM_EPS)
    return _split_dot_lhs(inv, _head_indicator(True), 2)


class _Tok:
    def __init__(self, prompt, tm):
        self.prompt = prompt
        if prompt:
            self.grid = (BATCH, SEQ // tm)
            self.rows = (1, tm)
            self.mod_block = (SUBLANE, D_MODEL)
            self.mod_row = PROMPT_ROW_BLOCK
        else:
            self.grid = (1, 1)
            self.rows = (DEC_SEQ, DEC_BATCH)
            self.mod_block = (DEC_BATCH, D_MODEL)
            self.mod_row = 0

    def x(self, n, col=0):
        if self.prompt:
            return pl.BlockSpec(self.rows + (n,), lambda b, i: (b, i, col))
        return pl.BlockSpec(self.rows + (n,), lambda b, i: (0, 0, col))

    def mod(self, col):
        row = self.mod_row
        return pl.BlockSpec(self.mod_block, lambda b, i: (row, col))

    def shape(self, n):
        return (BATCH, SEQ, n) if self.prompt else (DEC_SEQ, DEC_BATCH, n)


def _resident(shape):
    nd = len(shape)
    return pl.BlockSpec(shape, lambda *_: (0,) * nd, pipeline_mode=pl.Buffered(1))


def _resident_at(shape, lead):
    nd = len(shape) - len(lead)
    block = (None,) * len(lead) + tuple(shape[len(lead):])
    return pl.BlockSpec(block, lambda *_: tuple(lead) + (0,) * nd, pipeline_mode=pl.Buffered(1))


def _params():
    return pltpu.CompilerParams(dimension_semantics=("arbitrary", "arbitrary"), vmem_limit_bytes=VMEM_LIMIT)


def _ada_kernel(c_ref, w_ref, b_ref, o_ref):
    sc = jax.nn.silu(c_ref[...]).astype(BF16)
    o_ref[...] = _dot(sc, w_ref[...].astype(BF16)) + b_ref[...]


def _ada(c_all, w, b, name):
    layers, _, n = w.shape
    nj = n // ADA_TN
    return pl.pallas_call(
        _ada_kernel,
        grid=(layers, nj),
        in_specs=[pl.BlockSpec((MOD_ROWS, D_MODEL), lambda l, j: (0, 0)),
                  pl.BlockSpec((None, D_MODEL, ADA_TN), lambda l, j: (l, 0, j)),
                  pl.BlockSpec((None, 1, ADA_TN), lambda l, j: (l, 0, j))],
        out_specs=pl.BlockSpec((MOD_ROWS, ADA_TN), lambda l, j: (0, l * nj + j)),
        out_shape=jax.ShapeDtypeStruct((MOD_ROWS, layers * n), F32),
        compiler_params=_params(),
        name=name,
    )(c_all, w, b)


def _ffn_kernel(x_ref, sh_ref, sc_ref, gt_ref, g_ref, wu_ref, wd_ref, o_ref, *, prompt):
    x = x_ref[...]
    g_, r_, _ = x.shape
    n = g_ * r_
    u = _modulate(x, g_ref[...], _get_mod(sh_ref, prompt), _get_mod(sc_ref, prompt))
    u = u.reshape(n, D_MODEL).astype(BF16)
    acc = None
    for c in range(D_FF // FF_CHUNK):
        lo = c * FF_CHUNK
        a = _dot(u, wu_ref[:, lo:lo + FF_CHUNK])
        b = _dot(u, wu_ref[:, D_FF + lo:D_FF + lo + FF_CHUNK])
        gated = (jax.nn.silu(a) * b).astype(BF16)
        part = _dot(gated, wd_ref[lo:lo + FF_CHUNK, :])
        acc = part if acc is None else acc + part
    o_ref[...] = x + (0.5 * _get_mod(gt_ref, prompt)) * acc.reshape(g_, r_, D_MODEL)


def _ffn(tok, x, mod_all, layer, sub, g, wu, wd, name):
    which = (layer, sub // 2)
    return pl.pallas_call(
        functools.partial(_ffn_kernel, prompt=tok.prompt),
        grid=tok.grid,
        in_specs=[tok.x(D_MODEL),
                  tok.mod(_mod_col(layer, sub, 0)), tok.mod(_mod_col(layer, sub, 1)), tok.mod(_mod_col(layer, sub, 2)),
                  _resident((1, D_MODEL)), _resident_at(wu.shape, which), _resident_at(wd.shape, which)],
        out_specs=tok.x(D_MODEL),
        out_shape=jax.ShapeDtypeStruct(x.shape, F32),
        compiler_params=_params(),
        name=name,
    )(x, mod_all, mod_all, mod_all, g, wu, wd)


def _gla_in(u, wqkvr_ref, wglr_ref, wg2_ref, bg_ref):
    proj = _dot(u, wqkvr_ref[...])
    glr = _dot(u, wglr_ref[...])
    xg = _dot(glr.astype(BF16), wg2_ref[...]) + bg_ref[...]
    return proj, _log_sigmoid(xg) * (1.0 / GLA_TAU)


def _gla_out(o, r, gout, wout_ref):
    heads = []
    for h in range(GLA_HEADS):
        oh = o[:, h * GLA_DV:(h + 1) * GLA_DV]
        heads.append(_rms(oh) * gout)
    y = (jnp.concatenate(heads, axis=-1) * jax.nn.silu(r)).astype(BF16)
    return _dot(y, wout_ref[...])


def _gla_prompt_kernel(x_ref, sh_ref, sc_ref, gt_ref, g_ref, wqkvr_ref, wglr_ref, wg2_ref, bg_ref, gout_ref,
                       wout_ref, s0_ref, o_ref, sout_ref, proj_scr, la_scr, oscan_scr, st_scr):
    i = pl.program_id(1)
    tm = x_ref.shape[1]
    x = x_ref[0]
    u = _modulate(x, g_ref[...], _get_mod(sh_ref, True), _get_mod(sc_ref, True)).astype(BF16)
    proj, log_a = _gla_in(u, wqkvr_ref, wglr_ref, wg2_ref, bg_ref)
    proj_scr[...] = proj
    la_scr[...] = log_a

    @pl.when(i == 0)
    def _():
        for h in range(GLA_HEADS):
            st_scr[h] = s0_ref[0, h].T

    tri = jnp.where(_lower_tri(GLA_CHUNK), 1.0, 0.0).astype(BF16)
    causal = _lower_tri(GLA_CHUNK)
    qscale = GLA_DK ** -0.5

    states = [st_scr[h] for h in range(GLA_HEADS)]
    for c in range(tm // GLA_CHUNK):
        rows = pl.ds(c * GLA_CHUNK, GLA_CHUNK)
        bcum = _split_dot(tri, la_scr[rows, :], 3)
        blast = bcum[GLA_CHUNK - 1:GLA_CHUNK, :]
        bmid = bcum[GLA_CHUNK // 2 - 1:GLA_CHUNK // 2, :]
        q = proj_scr[rows, 0:DK_ALL] * qscale
        k = proj_scr[rows, DK_ALL:2 * DK_ALL]
        qe = (q * jnp.exp(bcum)).astype(BF16)
        qmid = (q * jnp.exp(bcum - bmid)).astype(BF16)
        kmid = (k * jnp.exp(bmid - bcum)).astype(BF16)
        kd = (k * jnp.exp(blast - bcum)).astype(BF16)
        elast = jnp.exp(blast)
        for h in range(GLA_HEADS):
            ks = slice(h * GLA_DK, (h + 1) * GLA_DK)
            vlo = 2 * DK_ALL + h * GLA_DV
            v = proj_scr[rows, vlo:vlo + GLA_DV].astype(BF16)
            st = states[h]
            att = jnp.where(causal, _dot_nt(qmid[:, ks], kmid[:, ks]), 0.0).astype(BF16)
            oscan_scr[rows, h * GLA_DV:(h + 1) * GLA_DV] = _dot_nt(qe[:, ks], st.astype(BF16)) + _dot(att, v)
            states[h] = st * elast[:, ks] + _dot_tn(v, kd[:, ks])
    for h in range(GLA_HEADS):
        st_scr[h] = states[h]

    r = proj_scr[:, 2 * DK_ALL + DV_ALL:QKVR]
    mix = _gla_out(oscan_scr[...], r, gout_ref[...], wout_ref)
    o_ref[0] = x + _get_mod(gt_ref, True) * mix

    @pl.when(i == pl.num_programs(1) - 1)
    def _():
        for h in range(GLA_HEADS):
            sout_ref[0, h] = st_scr[h].T


def _gla_prompt(x, mod_all, g, wqkvr, wglr, wg2, bg, gout, wout, s0):
    tok = _Tok(True, TM_GLA)
    state_spec = pl.BlockSpec((1, GLA_HEADS, GLA_DK, GLA_DV), lambda b, i: (b, 0, 0, 0))
    return pl.pallas_call(
        _gla_prompt_kernel,
        grid=tok.grid,
        in_specs=[tok.x(D_MODEL), tok.mod(_mod_col(0, 1, 0)), tok.mod(_mod_col(0, 1, 1)), tok.mod(_mod_col(0, 1, 2)),
                  _resident((1, D_MODEL)), _resident(wqkvr.shape), _resident(wglr.shape), _resident(wg2.shape),
                  _resident(bg.shape), _resident(gout.shape), _resident(wout.shape), state_spec],
        out_specs=[tok.x(D_MODEL), state_spec],
        out_shape=[jax.ShapeDtypeStruct(x.shape, F32), jax.ShapeDtypeStruct(s0.shape, F32)],
        scratch_shapes=[pltpu.VMEM((TM_GLA, QKVR), F32), pltpu.VMEM((TM_GLA, DK_ALL), F32),
                        pltpu.VMEM((TM_GLA, DV_ALL), F32), pltpu.VMEM((GLA_HEADS, GLA_DV, GLA_DK), F32)],
        compiler_params=_params(),
        name="gla_prompt",
    )(x, mod_all, mod_all, mod_all, g, wqkvr, wglr, wg2, bg, gout, wout, s0)


def _gla_in_sample_kernel(x_ref, sh_ref, sc_ref, g_ref, wqkvr_ref, wglr_ref, wg2_ref, bg_ref, proj_ref, la_ref):
    x = x_ref[...]
    g_, r_, _ = x.shape
    u = _modulate(x, g_ref[...], _get_mod(sh_ref, False), _get_mod(sc_ref, False))
    u = u.reshape(g_ * r_, D_MODEL).astype(BF16)
    proj, log_a = _gla_in(u, wqkvr_ref, wglr_ref, wg2_ref, bg_ref)
    proj_ref[...] = proj.reshape(g_, r_, QKVR)
    la_ref[...] = log_a.reshape(g_, r_, DK_ALL)


def _gla_scan_sample_kernel(proj_ref, la_ref, s0_ref, o_ref, sout_ref, xt_scr):
    qscale = GLA_DK ** -0.5
    n_kind = DEC_SEQ * GLA_HEADS
    xt_scr[...] = jnp.zeros_like(xt_scr)

    def seq(j, carry):
        for t in range(DEC_SEQ):
            a_t = jnp.exp(la_ref[t, pl.ds(j, 1), :])
            q_t = proj_ref[t, pl.ds(j, 1), 0:DK_ALL] * qscale
            k_t = proj_ref[t, pl.ds(j, 1), DK_ALL:2 * DK_ALL]
            for h in range(GLA_HEADS):
                ks = slice(h * GLA_DK, (h + 1) * GLA_DK)
                row = h * DEC_SEQ + t
                xt_scr[row:row + 1, :] = a_t[:, ks]
                xt_scr[n_kind + row:n_kind + row + 1, :] = q_t[:, ks]
                xt_scr[2 * n_kind + row:2 * n_kind + row + 1, :] = k_t[:, ks]
        xt = xt_scr[...].T
        for h in range(GLA_HEADS):
            s = s0_ref[j, h]
            for t in range(DEC_SEQ):
                row = h * DEC_SEQ + t
                a_c = xt[:, row:row + 1]
                q_c = xt[:, n_kind + row:n_kind + row + 1]
                k_c = xt[:, 2 * n_kind + row:2 * n_kind + row + 1]
                vlo = 2 * DK_ALL + h * GLA_DV
                v_t = proj_ref[t, pl.ds(j, 1), vlo:vlo + GLA_DV]
                s = a_c * s + k_c * v_t
                o_ref[t, pl.ds(j, 1), h * GLA_DV:(h + 1) * GLA_DV] = jnp.sum(q_c * s, axis=0, keepdims=True)
            sout_ref[j, h] = s
        return carry

    lax.fori_loop(0, SCAN_BS, seq, 0)


def _gla_out_sample_kernel(x_ref, gt_ref, oscan_ref, r_ref, gout_ref, wout_ref, o_ref):
    x = x_ref[...]
    g_, r_, _ = x.shape
    n = g_ * r_
    mix = _gla_out(oscan_ref[...].reshape(n, DV_ALL), r_ref[...].reshape(n, DV_ALL), gout_ref[...], wout_ref)
    o_ref[...] = x + _get_mod(gt_ref, False) * mix.reshape(g_, r_, D_MODEL)


def _gla_sample(x, mod_all, g, wqkvr, wglr, wg2, bg, gout, wout, s0):
    tok = _Tok(False, 0)
    proj, log_a = pl.pallas_call(
        _gla_in_sample_kernel,
        grid=tok.grid,
        in_specs=[tok.x(D_MODEL), tok.mod(_mod_col(0, 1, 0)), tok.mod(_mod_col(0, 1, 1)),
                  _resident((1, D_MODEL)), _resident(wqkvr.shape), _resident(wglr.shape), _resident(wg2.shape),
                  _resident(bg.shape)],
        out_specs=[tok.x(QKVR), tok.x(DK_ALL)],
        out_shape=[jax.ShapeDtypeStruct(tok.shape(QKVR), F32), jax.ShapeDtypeStruct(tok.shape(DK_ALL), F32)],
        compiler_params=_params(),
        name="gla_in_sample",
    )(x, mod_all, mod_all, g, wqkvr, wglr, wg2, bg)

    state_spec = pl.BlockSpec((SCAN_BS, GLA_HEADS, GLA_DK, GLA_DV), lambda j: (j, 0, 0, 0))
    oscan, s_out = pl.pallas_call(
        _gla_scan_sample_kernel,
        grid=(DEC_BATCH // SCAN_BS,),
        in_specs=[pl.BlockSpec((DEC_SEQ, SCAN_BS, QKVR), lambda j: (0, j, 0)),
                  pl.BlockSpec((DEC_SEQ, SCAN_BS, DK_ALL), lambda j: (0, j, 0)),
                  state_spec],
        out_specs=[pl.BlockSpec((DEC_SEQ, SCAN_BS, DV_ALL), lambda j: (0, j, 0)), state_spec],
        out_shape=[jax.ShapeDtypeStruct(tok.shape(DV_ALL), F32), jax.ShapeDtypeStruct(s0.shape, F32)],
        scratch_shapes=[pltpu.VMEM((LANE, LANE), F32)],
        compiler_params=pltpu.CompilerParams(dimension_semantics=("arbitrary",), vmem_limit_bytes=VMEM_LIMIT),
        name="gla_scan_sample",
    )(proj, log_a, s0)

    h = pl.pallas_call(
        _gla_out_sample_kernel,
        grid=tok.grid,
        in_specs=[tok.x(D_MODEL), tok.mod(_mod_col(0, 1, 2)), tok.x(DV_ALL),
                  tok.x(DV_ALL, col=(2 * DK_ALL + DV_ALL) // DV_ALL), _resident(gout.shape), _resident(wout.shape)],
        out_specs=tok.x(D_MODEL),
        out_shape=jax.ShapeDtypeStruct(x.shape, F32),
        compiler_params=_params(),
        name="gla_out_sample",
    )(x, mod_all, oscan, proj, gout, wout)
    return h, s_out


def _kv_kernel(x_ref, sh_ref, sc_ref, g_ref, wkv_ref, wf_ref, bf_ref, gk_ref, *refs, prompt):
    if prompt:
        k_ref, v_ref, lf_ref, kb_ref, vb_ref, f_ref, ft_ref, carry_scr = refs
    else:
        k_ref, v_ref, lf_ref = refs
    x = x_ref[...]
    g_, r_, _ = x.shape
    n = g_ * r_
    u = _modulate(x, g_ref[...], _get_mod(sh_ref, prompt), _get_mod(sc_ref, prompt))
    u = u.reshape(n, D_MODEL).astype(BF16)
    kv = _dot(u, wkv_ref[...])
    k = kv[:, :D_MODEL]
    v = kv[:, D_MODEL:]
    kn = k * _head_inv_rms(k) * gk_ref[...]
    lf = _log_sigmoid(_dot(u, wf_ref[...]) + bf_ref[...])
    k_ref[...] = kn.reshape(g_, r_, D_MODEL)
    v_ref[...] = v.reshape(g_, r_, D_MODEL)
    lf_ref[...] = lf[:, :FOX_HEADS].reshape(g_, r_, FOX_HEADS)
    if prompt:
        kb_ref[...] = kn.astype(BF16).reshape(g_, r_, D_MODEL)
        vb_ref[...] = v.astype(BF16).reshape(g_, r_, D_MODEL)

        @pl.when(pl.program_id(1) == 0)
        def _():
            carry_scr[...] = jnp.zeros_like(carry_scr)

        tri = jnp.where(_lower_tri(n), 1.0, 0.0).astype(BF16)
        fsum = _split_dot(tri, lf, 3) + carry_scr[...]
        carry_scr[...] = fsum[n - 1:n, :]
        f_ref[...] = fsum[:, :FOX_HEADS].reshape(g_, r_, FOX_HEADS)
        ft_ref[0] = fsum.T[:FOX_HEADS, :]


def _kv(tok, x, mod_kv, g, wkv, wf, bf, gk, name):
    prompt = tok.prompt
    out_specs = [tok.x(D_MODEL), tok.x(D_MODEL), tok.x(FOX_HEADS)]
    out_shape = [jax.ShapeDtypeStruct(tok.shape(D_MODEL), F32), jax.ShapeDtypeStruct(tok.shape(D_MODEL), F32),
                 jax.ShapeDtypeStruct(tok.shape(FOX_HEADS), F32)]
    scratch = []
    if prompt:
        out_specs += [tok.x(D_MODEL), tok.x(D_MODEL), tok.x(FOX_HEADS),
                      pl.BlockSpec((1, FOX_HEADS, TM_KV), lambda b, i: (b, 0, i))]
        out_shape += [jax.ShapeDtypeStruct(tok.shape(D_MODEL), BF16), jax.ShapeDtypeStruct(tok.shape(D_MODEL), BF16),
                      jax.ShapeDtypeStruct(tok.shape(FOX_HEADS), F32),
                      jax.ShapeDtypeStruct((BATCH, FOX_HEADS, SEQ), F32)]
        scratch = [pltpu.VMEM((1, LANE), F32)]
    return pl.pallas_call(
        functools.partial(_kv_kernel, prompt=prompt),
        grid=tok.grid,
        in_specs=[tok.x(D_MODEL), tok.mod(0), tok.mod(1), _resident((1, D_MODEL)), _resident(wkv.shape),
                  _resident(wf.shape), _resident(bf.shape), _resident(gk.shape)],
        out_specs=out_specs,
        out_shape=out_shape,
        scratch_shapes=scratch,
        compiler_params=_params(),
        name=name,
    )(x, mod_kv, mod_kv, g, wkv, wf, bf, gk)


def _fox_q(u, wqg_ref, gq_ref):
    qg = _dot(u, wqg_ref[...])
    q = qg[:, :D_MODEL]
    og = qg[:, D_MODEL:]
    qn = q * _head_inv_rms(q) * gq_ref[...] * (FOX_HD ** -0.5 * LOG2E)
    return qn, og


def _softmax_step(t, row_bias, m, l, acc, v):
    m_cur = jnp.max(t, axis=-1, keepdims=True)
    if row_bias is not None:
        m_cur = m_cur + row_bias
    m_new = jnp.maximum(m, m_cur)
    alpha = jnp.exp2(m - m_new)
    shift = -m_new if row_bias is None else row_bias - m_new
    p = jnp.exp2(t + jnp.tile(shift, (1, t.shape[1] // LANE)))
    l_new = alpha * l + jnp.sum(p, axis=-1, keepdims=True)
    pv = _dot(p.astype(BF16), v)
    acc_new = jnp.tile(alpha, (1, pv.shape[1] // LANE)) * acc + pv
    return m_new, l_new, acc_new


def _fox_prompt_kernel(x_ref, sh_ref, sc_ref, gt_ref, g_ref, wqg_ref, gq_ref, wo_ref, kb_ref, vb_ref, f_ref, ft_ref,
                       o_ref, qm_scr, og_scr, oatt_scr, m_scr, l_scr, acc_scr, fq_scr):
    i = pl.program_id(1)
    tq = x_ref.shape[1]
    x = x_ref[0]
    u = _modulate(x, g_ref[...], _get_mod(sh_ref, True), _get_mod(sc_ref, True)).astype(BF16)
    qn, og = _fox_q(u, wqg_ref, gq_ref)
    og_scr[...] = og
    first = lax.broadcasted_iota(jnp.int32, (tq, LANE), 1) < FOX_HD

    for pair in range(FOX_HEADS // 2):
        q2 = qn[:, pair * LANE:(pair + 1) * LANE]
        qm_scr[2 * pair] = jnp.where(first, q2, 0.0).astype(BF16)
        qm_scr[2 * pair + 1] = jnp.where(first, 0.0, q2).astype(BF16)

    fq_all = f_ref[0]
    causal = _lower_tri(tq)

    for grp in range(FOX_HEADS // FOX_GROUP):
        m_scr[...] = jnp.full_like(m_scr, NEG)
        l_scr[...] = jnp.zeros_like(l_scr)
        acc_scr[...] = jnp.zeros_like(acc_scr)
        for idx in range(FOX_GROUP):
            head = grp * FOX_GROUP + idx
            fq_scr[idx] = jnp.broadcast_to(fq_all[:, head:head + 1], (tq, LANE)) * LOG2E

        def step(j, diagonal, grp=grp):
            rows = pl.ds(pl.multiple_of(j * tq, tq), tq)
            for idx in range(FOX_GROUP):
                head = grp * FOX_GROUP + idx
                cols = slice((head // 2) * LANE, (head // 2 + 1) * LANE)
                fk = ft_ref[0, head, pl.ds(j, 1), :] * LOG2E
                t = _dot_nt(qm_scr[head], kb_ref[0, rows, cols]) - fk
                if diagonal:
                    t = jnp.where(causal, t, NEG)
                m, l, acc = _softmax_step(t, fq_scr[idx], m_scr[idx], l_scr[idx], acc_scr[idx],
                                          vb_ref[0, rows, cols])
                m_scr[idx] = m
                l_scr[idx] = l
                acc_scr[idx] = acc

        def off_diagonal(j, carry):
            step(j, False)
            return carry

        lax.fori_loop(0, i, off_diagonal, 0)
        step(i, True)
        for pp in range(FOX_GROUP // 2):
            pair = grp * (FOX_GROUP // 2) + pp
            o2 = jnp.where(first, acc_scr[2 * pp] / l_scr[2 * pp], acc_scr[2 * pp + 1] / l_scr[2 * pp + 1])
            oatt_scr[:, pair * LANE:(pair + 1) * LANE] = o2

    gated = (oatt_scr[...] * jax.nn.sigmoid(og_scr[...])).astype(BF16)
    o_ref[0] = x + _get_mod(gt_ref, True) * _dot(gated, wo_ref[...])


def _fox_prompt(x, mod_all, g, wqg, gq, wo, kb, vb, fsum, fsum_t):
    tok = _Tok(True, TQ_FOX)
    nk = SEQ // TQ_FOX
    seq_spec = pl.BlockSpec((1, SEQ, D_MODEL), lambda b, i: (b, 0, 0), pipeline_mode=pl.Buffered(1))
    return pl.pallas_call(
        _fox_prompt_kernel,
        grid=tok.grid,
        in_specs=[tok.x(D_MODEL), tok.mod(_mod_col(1, 1, 0)), tok.mod(_mod_col(1, 1, 1)), tok.mod(_mod_col(1, 1, 2)),
                  _resident((1, D_MODEL)), _resident(wqg.shape), _resident(gq.shape), _resident(wo.shape),
                  seq_spec, seq_spec, tok.x(FOX_HEADS),
                  pl.BlockSpec((1, FOX_HEADS, nk, TQ_FOX), lambda b, i: (b, 0, 0, 0))],
        out_specs=tok.x(D_MODEL),
        out_shape=jax.ShapeDtypeStruct(x.shape, F32),
        scratch_shapes=[pltpu.VMEM((FOX_HEADS, TQ_FOX, LANE), BF16), pltpu.VMEM((TQ_FOX, D_MODEL), F32),
                        pltpu.VMEM((TQ_FOX, D_MODEL), F32), pltpu.VMEM((FOX_GROUP, TQ_FOX, LANE), F32),
                        pltpu.VMEM((FOX_GROUP, TQ_FOX, LANE), F32), pltpu.VMEM((FOX_GROUP, TQ_FOX, LANE), F32),
                        pltpu.VMEM((FOX_GROUP, TQ_FOX, LANE), F32)],
        compiler_params=_params(),
        name="fox_prompt",
    )(x, mod_all, mod_all, mod_all, g, wqg, gq, wo, kb, vb, fsum, fsum_t.reshape(BATCH, FOX_HEADS, nk, TQ_FOX))


def _fox_q_sample_kernel(x_ref, sh_ref, sc_ref, g_ref, wqg_ref, gq_ref, q_ref, og_ref):
    x = x_ref[...]
    g_, r_, _ = x.shape
    u = _modulate(x, g_ref[...], _get_mod(sh_ref, False), _get_mod(sc_ref, False))
    qn, og = _fox_q(u.reshape(g_ * r_, D_MODEL).astype(BF16), wqg_ref, gq_ref)
    q_ref[...] = qn.reshape(g_, r_, D_MODEL)
    og_ref[...] = og.reshape(g_, r_, D_MODEL)


def _fox_out_sample_kernel(x_ref, gt_ref, oatt_ref, og_ref, wo_ref, o_ref):
    x = x_ref[...]
    g_, r_, _ = x.shape
    n = g_ * r_
    gated = (oatt_ref[...] * jax.nn.sigmoid(og_ref[...])).reshape(n, D_MODEL).astype(BF16)
    o_ref[...] = x + _get_mod(gt_ref, False) * _dot(gated, wo_ref[...]).reshape(g_, r_, D_MODEL)


def _paged_kernel(pt_ref, q_ref, kn_ref, vn_ref, lnt_ref, *refs):
    del pt_ref
    k_pages = refs[:N_PAGES]
    v_pages = refs[N_PAGES:2 * N_PAGES]
    l_pages = refs[2 * N_PAGES:3 * N_PAGES]
    o_ref = refs[3 * N_PAGES]
    kn_scr, vn_scr = refs[3 * N_PAGES + 1:]
    nrow = DEC_SEQ * FOX_HEADS

    hrow = lax.broadcasted_iota(jnp.int32, (FOX_HEADS, D_MODEL), 0)
    hcol = lax.broadcasted_iota(jnp.int32, (FOX_HEADS, D_MODEL), 1)
    head_mask = (hcol >> HD_SHIFT) == hrow
    q = q_ref[0]
    qbd = jnp.concatenate(
        [jnp.where(head_mask, jnp.broadcast_to(q[t:t + 1, :], (FOX_HEADS, D_MODEL)), 0.0) for t in range(DEC_SEQ)],
        axis=0).astype(BF16)

    lnt = lnt_ref[0]
    cn = [lnt[:, 0:1]]
    for t in range(1, DEC_SEQ):
        cn.append(cn[-1] + lnt[:, t:t + 1])
    cn_col = jnp.concatenate(cn, axis=0)

    ri = lax.broadcasted_iota(jnp.int32, (PAGE_SIZE, 2 * LANE), 0)
    ci = lax.broadcasted_iota(jnp.int32, (PAGE_SIZE, 2 * LANE), 1)
    suffix = jnp.where(((ci < PAGE_SIZE) & (ri > ci)) | (ci == PAGE_SIZE), 1.0, 0.0).astype(BF16)

    logits = [None] * N_PAGES
    tot = jnp.zeros((FOX_HEADS, 1), F32)
    for idx in reversed(range(N_PAGES)):
        sums = _split_dot_lhs(l_pages[idx][0], suffix, 2)
        bias16 = sums[:, :PAGE_SIZE] + tot
        tot = tot + sums[:, PAGE_SIZE:PAGE_SIZE + 1]
        bias = jnp.concatenate([bias16] * DEC_SEQ, axis=0) + cn_col
        logits[idx] = _dot(qbd, k_pages[idx][0].astype(BF16)) + bias * LOG2E

    kn_scr[...] = jnp.zeros_like(kn_scr)
    vn_scr[...] = jnp.zeros_like(vn_scr)
    kn_scr[0:DEC_SEQ, :] = kn_ref[0]
    vn_scr[0:DEC_SEQ, :] = vn_ref[0]
    lane = lax.broadcasted_iota(jnp.int32, (nrow, LANE), 1)
    step_of_row = lax.broadcasted_iota(jnp.int32, (nrow, LANE), 0) >> HEAD_SHIFT
    cn_keys = jnp.zeros((nrow, LANE), F32)
    for t in range(DEC_SEQ):
        cn_keys = jnp.where(lane == t, jnp.concatenate([cn[t]] * DEC_SEQ, axis=0), cn_keys)
    s_new = _dot_nt(qbd, kn_scr[...].astype(BF16)) + (cn_col - cn_keys) * LOG2E
    logits.append(jnp.where(lane <= step_of_row, s_new, NEG))

    m = logits[0]
    for s in logits[1:]:
        m = jnp.maximum(m, s)
    m = jnp.broadcast_to(jnp.max(m, axis=-1, keepdims=True), (nrow, LANE))
    l = jnp.zeros((nrow, LANE), F32)
    acc = jnp.zeros((nrow, D_MODEL), F32)
    for idx, s in enumerate(logits):
        p = jnp.exp2(s - m)
        l = l + p
        if idx < N_PAGES:
            acc = acc + _dot_nt(p.astype(BF16), v_pages[idx][0].astype(BF16))
        else:
            acc = acc + _dot(p.astype(BF16), vn_scr[...].astype(BF16))
    l = jnp.broadcast_to(jnp.sum(l, axis=-1, keepdims=True), (nrow, LANE))
    out = acc / jnp.tile(l, (1, D_MODEL // LANE))
    for t in range(DEC_SEQ):
        blk = out[t * FOX_HEADS:(t + 1) * FOX_HEADS, :]
        o_ref[0, t:t + 1, :] = jnp.sum(jnp.where(head_mask, blk, 0.0), axis=0, keepdims=True)


def _paged_attend(q, k_new, v_new, lf_new_t, cache_k_t, cache_v_t, cache_lf_t, page_table):
    def page_spec(shape, idx):
        return pl.BlockSpec((1,) + shape, lambda b, pt: (pt[b, idx], 0, 0))

    seq_spec = pl.BlockSpec((1, DEC_SEQ, D_MODEL), lambda b, pt: (b, 0, 0))
    in_specs = [seq_spec, seq_spec, seq_spec, pl.BlockSpec((1, FOX_HEADS, DEC_SEQ), lambda b, pt: (b, 0, 0))]
    in_specs += [page_spec((D_MODEL, PAGE_SIZE), i) for i in range(N_PAGES)]
    in_specs += [page_spec((D_MODEL, PAGE_SIZE), i) for i in range(N_PAGES)]
    in_specs += [page_spec((FOX_HEADS, PAGE_SIZE), i) for i in range(N_PAGES)]
    grid_spec = pltpu.PrefetchScalarGridSpec(
        num_scalar_prefetch=1,
        grid=(DEC_BATCH,),
        in_specs=in_specs,
        out_specs=seq_spec,
        scratch_shapes=[pltpu.VMEM((PAGE_SIZE, D_MODEL), F32), pltpu.VMEM((PAGE_SIZE, D_MODEL), F32)],
    )
    return pl.pallas_call(
        _paged_kernel,
        grid_spec=grid_spec,
        out_shape=jax.ShapeDtypeStruct((DEC_BATCH, DEC_SEQ, D_MODEL), F32),
        compiler_params=pltpu.CompilerParams(dimension_semantics=("arbitrary",), vmem_limit_bytes=VMEM_LIMIT),
        name="fox_paged",
    )(page_table, q, k_new, v_new, lf_new_t, *([cache_k_t] * N_PAGES), *([cache_v_t] * N_PAGES),
      *([cache_lf_t] * N_PAGES))


def _fox_sample(x, mod_all, g, wqg, gq, wo, k_new, v_new, lf_new, cache_k, cache_v, cache_logf, page_table):
    tok = _Tok(False, 0)
    qn, og = pl.pallas_call(
        _fox_q_sample_kernel,
        grid=tok.grid,
        in_specs=[tok.x(D_MODEL), tok.mod(_mod_col(1, 1, 0)), tok.mod(_mod_col(1, 1, 1)), _resident((1, D_MODEL)),
                  _resident(wqg.shape), _resident(gq.shape)],
        out_specs=[tok.x(D_MODEL), tok.x(D_MODEL)],
        out_shape=[jax.ShapeDtypeStruct(x.shape, F32), jax.ShapeDtypeStruct(x.shape, F32)],
        compiler_params=_params(),
        name="fox_q_sample",
    )(x, mod_all, mod_all, g, wqg, gq)

    n_phys = cache_k.shape[0]
    oatt = _paged_attend(
        qn.transpose(1, 0, 2), k_new.transpose(1, 0, 2), v_new.transpose(1, 0, 2), lf_new.transpose(1, 2, 0),
        cache_k.transpose(0, 2, 3, 1).reshape(n_phys, D_MODEL, PAGE_SIZE),
        cache_v.transpose(0, 2, 3, 1).reshape(n_phys, D_MODEL, PAGE_SIZE),
        cache_logf.transpose(0, 2, 1), page_table)

    return pl.pallas_call(
        _fox_out_sample_kernel,
        grid=tok.grid,
        in_specs=[tok.x(D_MODEL), tok.mod(_mod_col(1, 1, 2)), tok.x(D_MODEL), tok.x(D_MODEL), _resident(wo.shape)],
        out_specs=tok.x(D_MODEL),
        out_shape=jax.ShapeDtypeStruct(x.shape, F32),
        compiler_params=_params(),
        name="fox_out_sample",
    )(x, mod_all, oatt.transpose(1, 0, 2), og, wo)


def kernel(x_prompt, x_sample, state_gla, cache_k, cache_v, cache_logf, page_table, c_prompt, c_sample, w_ada, b_ada, g_norm, w_ffn_up, w_ffn_down, gla_w_in, gla_w_gate2, gla_b_gate, gla_g_out, gla_w_out, w_ada_kv, b_ada_kv, g_kv, w_kvf, b_f, g_k, fox_w_qg, fox_g_q, fox_w_o):
    c_all = jnp.concatenate([c_sample, c_prompt, jnp.zeros((MOD_ROWS - DEC_BATCH - BATCH, D_MODEL), F32)], axis=0)
    wu = w_ffn_up.astype(BF16)
    wd = w_ffn_down.astype(BF16)
    w_in = gla_w_in[0]
    wqkvr = w_in[:, :QKVR].astype(BF16)
    wglr = jnp.pad(w_in[:, QKVR:], ((0, 0), (0, LANE - GLA_RANK))).astype(BF16)
    wg2 = jnp.pad(gla_w_gate2[0], ((0, LANE - GLA_RANK), (0, 0))).astype(BF16)
    bg = gla_b_gate[0][None, :]
    gout = gla_g_out[0][None, :]
    wout = gla_w_out[0].astype(BF16)
    wkv = w_kvf[:, :2 * D_MODEL].astype(BF16)
    wf = jnp.pad(w_kvf[:, 2 * D_MODEL:], ((0, 0), (0, LANE - FOX_HEADS))).astype(BF16)
    bf = jnp.pad(b_f, (0, LANE - FOX_HEADS))[None, :]
    gk = jnp.tile(g_k, FOX_HEADS)[None, :]
    wqg = fox_w_qg[0].astype(BF16)
    gq = jnp.tile(fox_g_q[0], FOX_HEADS)[None, :]
    wo = fox_w_o[0].astype(BF16)
    gkv = g_kv[None, :]

    def gn(layer, sub):
        return g_norm[layer, sub][None, :]

    mod_all = _ada(c_all, w_ada, b_ada[:, None, :], "ada_mod")
    mod_kv = _ada(c_all, w_ada_kv[None], b_ada_kv[None, None, :], "ada_mod_kv")

    tok = _Tok(True, TM_FFN)
    h = _ffn(tok, x_prompt, mod_all, 0, 0, gn(0, 0), wu, wd,"ffn_p00")
    s0 = jnp.zeros((BATCH, GLA_HEADS, GLA_DK, GLA_DV), F32)
    h, sg_prompt = _gla_prompt(h, mod_all, gn(0, 1), wqkvr, wglr, wg2, bg, gout, wout, s0)
    h = _ffn(tok, h, mod_all, 0, 2, gn(0, 2), wu, wd,"ffn_p02")
    k_p, v_p, lf_p, kb, vb, fsum, fsum_t = _kv(_Tok(True, TM_KV), h, mod_kv, gkv, wkv, wf, bf, gk, "kv_prompt")
    h = _ffn(tok, h, mod_all, 1, 0, gn(1, 0), wu, wd,"ffn_p10")
    h = _fox_prompt(h, mod_all, gn(1, 1), wqg, gq, wo, kb, vb, fsum, fsum_t)
    y_prompt = _ffn(tok, h, mod_all, 1, 2, gn(1, 2), wu, wd,"ffn_p12")

    tok = _Tok(False, 0)
    hs = x_sample.transpose(1, 0, 2)
    hs = _ffn(tok, hs, mod_all, 0, 0, gn(0, 0), wu, wd,"ffn_s00")
    hs, sg_sample = _gla_sample(hs, mod_all, gn(0, 1), wqkvr, wglr, wg2, bg, gout, wout, state_gla[0])
    hs = _ffn(tok, hs, mod_all, 0, 2, gn(0, 2), wu, wd,"ffn_s02")
    k_s, v_s, lf_s = _kv(tok, hs, mod_kv, gkv, wkv, wf, bf, gk, "kv_sample")
    hs = _ffn(tok, hs, mod_all, 1, 0, gn(1, 0), wu, wd,"ffn_s10")
    hs = _fox_sample(hs, mod_all, gn(1, 1), wqg, gq, wo, k_s, v_s, lf_s, cache_k, cache_v, cache_logf, page_table)
    hs = _ffn(tok, hs, mod_all, 1, 2, gn(1, 2), wu, wd,"ffn_s12")
    y_sample = hs.transpose(1, 0, 2)

    def heads(t, lead):
        return t.reshape(lead + (FOX_HEADS, FOX_HD))

    return (y_prompt, y_sample, sg_prompt[None],
            heads(k_p, (BATCH, SEQ)), heads(v_p, (BATCH, SEQ)), lf_p,
            sg_sample[None],
            heads(k_s.transpose(1, 0, 2), (DEC_BATCH, DEC_SEQ)), heads(v_s.transpose(1, 0, 2), (DEC_BATCH, DEC_SEQ)),
            lf_s.transpose(1, 0, 2))
```

```python
import functools

import jax
import jax.numpy as jnp
from jax import lax
from jax.experimental import pallas as pl
from jax.experimental.pallas import tpu as pltpu

F32 = jnp.float32
BF16 = jnp.bfloat16

D_MODEL = 1024
BATCH = 4
SEQ = 4096
DEC_BATCH = 128
DEC_SEQ = 4
PAST_LEN = 2048
PAGE_SIZE = 128
N_PAGES = PAST_LEN // PAGE_SIZE
GLA_HEADS = 4
GLA_DK = 128
GLA_DV = 256
GLA_RANK = 16
GLA_TAU = 16.0
FOX_HEADS = 16
FOX_HD = 64
D_FF = 2816
NORM_EPS = 1e-6
HD_SHIFT = FOX_HD.bit_length() - 1
HEAD_SHIFT = FOX_HEADS.bit_length() - 1

DK_ALL = GLA_HEADS * GLA_DK
DV_ALL = GLA_HEADS * GLA_DV
QKVR = 2 * DK_ALL + 2 * DV_ALL
LANE = 128
SUBLANE = 8

NEG = -1e30
LOG2E = 1.4426950408889634
VMEM_LIMIT = 56 * 1024 * 1024

TM_FFN = 512
FF_CHUNK = 1408
TM_GLA = 512
GLA_CHUNK = 128
TM_KV = 512
TQ_FOX = 512
FOX_GROUP = 4
ADA_TN = 1024
SCAN_BS = 8

MOD_ROWS = DEC_BATCH + SUBLANE
PROMPT_ROW_BLOCK = DEC_BATCH // SUBLANE


def _mod_col(layer, sub, kind):
    return layer * 9 + sub * 3 + kind


def _dot(a, b):
    return jnp.dot(a, b, preferred_element_type=F32)


def _dot_nt(a, b):
    return lax.dot_general(a, b, (((1,), (1,)), ((), ())), preferred_element_type=F32)


def _dot_tn(a, b):
    return lax.dot_general(a, b, (((0,), (0,)), ((), ())), preferred_element_type=F32)


def _split_dot(a_bf, x, terms):
    out = None
    r = x
    for _ in range(terms):
        p = r.astype(BF16)
        r = r - p.astype(F32)
        d = _dot(a_bf, p)
        out = d if out is None else out + d
    return out


def _split_dot_lhs(x, b_bf, terms):
    out = None
    r = x
    for _ in range(terms):
        p = r.astype(BF16)
        r = r - p.astype(F32)
        d = _dot(p, b_bf)
        out = d if out is None else out + d
    return out


def _log_sigmoid(x):
    return jnp.minimum(x, 0.0) - jnp.log1p(jnp.exp(-jnp.abs(x)))


def _rms(x):
    return x * lax.rsqrt(jnp.mean(x * x, axis=-1, keepdims=True) + NORM_EPS)


def _modulate(x, g, shift, scale):
    return (_rms(x) * g) * (1.0 + scale) + shift


def _get_mod(ref, prompt):
    if prompt:
        return ref[pl.ds(pl.program_id(0), 1), :]
    return ref[...]


def _lower_tri(n, strict=False):
    r = lax.broadcasted_iota(jnp.int32, (n, n), 0)
    c = lax.broadcasted_iota(jnp.int32, (n, n), 1)
    return (r > c) if strict else (r >= c)


def _head_indicator(transposed):
    if transposed:
        h = lax.broadcasted_iota(jnp.int32, (LANE, D_MODEL), 0)
        c = lax.broadcasted_iota(jnp.int32, (LANE, D_MODEL), 1)
    else:
        c = lax.broadcasted_iota(jnp.int32, (D_MODEL, LANE), 0)
        h = lax.broadcasted_iota(jnp.int32, (D_MODEL, LANE), 1)
    return jnp.where((c >> HD_SHIFT) == h, 1.0, 0.0).astype(BF16)


def _head_inv_rms(x):
    ss = _split_dot_lhs(x * x, _head_indicator(False), 2)
    inv = lax.rsqrt(ss * (1.0 / FOX_HD) + NORM_EPS)
    return _split_dot_lhs(inv, _head_indicator(True), 2)


class _Tok:
    def __init__(self, prompt, tm):
        self.prompt = prompt
        if prompt:
            self.grid = (BATCH, SEQ // tm)
            self.rows = (1, tm)
            self.mod_block = (SUBLANE, D_MODEL)
            self.mod_row = PROMPT_ROW_BLOCK
        else:
            self.grid = (1, 1)
            self.rows = (DEC_SEQ, DEC_BATCH)
            self.mod_block = (DEC_BATCH, D_MODEL)
            self.mod_row = 0

    def x(self, n, col=0):
        if self.prompt:
            return pl.BlockSpec(self.rows + (n,), lambda b, i: (b, i, col))
        return pl.BlockSpec(self.rows + (n,), lambda b, i: (0, 0, col))

    def mod(self, col):
        row = self.mod_row
        return pl.BlockSpec(self.mod_block, lambda b, i: (row, col))

    def shape(self, n):
        return (BATCH, SEQ, n) if self.prompt else (DEC_SEQ, DEC_BATCH, n)


def _resident(shape):
    nd = len(shape)
    return pl.BlockSpec(shape, lambda *_: (0,) * nd, pipeline_mode=pl.Buffered(1))


def _resident_at(shape, lead):
    nd = len(shape) - len(lead)
    block = (None,) * len(lead) + tuple(shape[len(lead):])
    return pl.BlockSpec(block, lambda *_: tuple(lead) + (0,) * nd, pipeline_mode=pl.Buffered(1))


def _params():
    return pltpu.CompilerParams(dimension_semantics=("arbitrary", "arbitrary"), vmem_limit_bytes=VMEM_LIMIT)


def _ada_kernel(c_ref, w_ref, b_ref, o_ref):
    sc = jax.nn.silu(c_ref[...]).astype(BF16)
    o_ref[...] = _dot(sc, w_ref[...].astype(BF16)) + b_ref[...]


def _ada(c_all, w, b, name):
    layers, _, n = w.shape
    nj = n // ADA_TN
    return pl.pallas_call(
        _ada_kernel,
        grid=(layers, nj),
        in_specs=[pl.BlockSpec((MOD_ROWS, D_MODEL), lambda l, j: (0, 0)),
                  pl.BlockSpec((None, D_MODEL, ADA_TN), lambda l, j: (l, 0, j)),
                  pl.BlockSpec((None, 1, ADA_TN), lambda l, j: (l, 0, j))],
        out_specs=pl.BlockSpec((MOD_ROWS, ADA_TN), lambda l, j: (0, l * nj + j)),
        out_shape=jax.ShapeDtypeStruct((MOD_ROWS, layers * n), F32),
        compiler_params=_params(),
        name=name,
    )(c_all, w, b)


def _ffn_kernel(x_ref, sh_ref, sc_ref, gt_ref, g_ref, wu_ref, wd_ref, o_ref, *, prompt):
    x = x_ref[...]
    g_, r_, _ = x.shape
    n = g_ * r_
    u = _modulate(x, g_ref[...], _get_mod(sh_ref, prompt), _get_mod(sc_ref, prompt))
    u = u.reshape(n, D_MODEL).astype(BF16)
    acc = None
    for c in range(D_FF // FF_CHUNK):
        lo = c * FF_CHUNK
        a = _dot(u, wu_ref[:, lo:lo + FF_CHUNK])
        b = _dot(u, wu_ref[:, D_FF + lo:D_FF + lo + FF_CHUNK])
        gated = (jax.nn.silu(a) * b).astype(BF16)
        part = _dot(gated, wd_ref[lo:lo + FF_CHUNK, :])
        acc = part if acc is None else acc + part
    o_ref[...] = x + (0.5 * _get_mod(gt_ref, prompt)) * acc.reshape(g_, r_, D_MODEL)


def _ffn(tok, x, mod_all, layer, sub, g, wu, wd, name):
    which = (layer, sub // 2)
    return pl.pallas_call(
        functools.partial(_ffn_kernel, prompt=tok.prompt),
        grid=tok.grid,
        in_specs=[tok.x(D_MODEL),
                  tok.mod(_mod_col(layer, sub, 0)), tok.mod(_mod_col(layer, sub, 1)), tok.mod(_mod_col(layer, sub, 2)),
                  _resident((1, D_MODEL)), _resident_at(wu.shape, which), _resident_at(wd.shape, which)],
        out_specs=tok.x(D_MODEL),
        out_shape=jax.ShapeDtypeStruct(x.shape, F32),
        compiler_params=_params(),
        name=name,
    )(x, mod_all, mod_all, mod_all, g, wu, wd)


def _gla_in(u, wqkvr_ref, wglr_ref, wg2_ref, bg_ref):
    proj = _dot(u, wqkvr_ref[...])
    glr = _dot(u, wglr_ref[...])
    xg = _dot(glr.astype(BF16), wg2_ref[...]) + bg_ref[...]
    return proj, _log_sigmoid(xg) * (1.0 / GLA_TAU)


def _gla_out(o, r, gout, wout_ref):
    heads = []
    for h in range(GLA_HEADS):
        oh = o[:, h * GLA_DV:(h + 1) * GLA_DV]
        heads.append(_rms(oh) * gout)
    y = (jnp.concatenate(heads, axis=-1) * jax.nn.silu(r)).astype(BF16)
    return _dot(y, wout_ref[...])


def _gla_prompt_kernel(x_ref, sh_ref, sc_ref, gt_ref, g_ref, wqkvr_ref, wglr_ref, wg2_ref, bg_ref, gout_ref,
                       wout_ref, s0_ref, o_ref, sout_ref, proj_scr, la_scr, oscan_scr, st_scr):
    i = pl.program_id(1)
    tm = x_ref.shape[1]
    x = x_ref[0]
    u = _modulate(x, g_ref[...], _get_mod(sh_ref, True), _get_mod(sc_ref, True)).astype(BF16)
    proj, log_a = _gla_in(u, wqkvr_ref, wglr_ref, wg2_ref, bg_ref)
    proj_scr[...] = proj
    la_scr[...] = log_a

    @pl.when(i == 0)
    def _():
        for h in range(GLA_HEADS):
            st_scr[h] = s0_ref[0, h].T

    tri = jnp.where(_lower_tri(GLA_CHUNK), 1.0, 0.0).astype(BF16)
    causal = _lower_tri(GLA_CHUNK)
    qscale = GLA_DK ** -0.5

    states = [st_scr[h] for h in range(GLA_HEADS)]
    for c in range(tm // GLA_CHUNK):
        rows = pl.ds(c * GLA_CHUNK, GLA_CHUNK)
        bcum = _split_dot(tri, la_scr[rows, :], 3)
        blast = bcum[GLA_CHUNK - 1:GLA_CHUNK, :]
        bmid = bcum[GLA_CHUNK // 2 - 1:GLA_CHUNK // 2, :]
        q = proj_scr[rows, 0:DK_ALL] * qscale
        k = proj_scr[rows, DK_ALL:2 * DK_ALL]
        qe = (q * jnp.exp(bcum)).astype(BF16)
        qmid = (q * jnp.exp(bcum - bmid)).astype(BF16)
        kmid = (k * jnp.exp(bmid - bcum)).astype(BF16)
        kd = (k * jnp.exp(blast - bcum)).astype(BF16)
        elast = jnp.exp(blast)
        for h in range(GLA_HEADS):
            ks = slice(h * GLA_DK, (h + 1) * GLA_DK)
            vlo = 2 * DK_ALL + h * GLA_DV
            v = proj_scr[rows, vlo:vlo + GLA_DV].astype(BF16)
            st = states[h]
            att = jnp.where(causal, _dot_nt(qmid[:, ks], kmid[:, ks]), 0.0).astype(BF16)
            oscan_scr[rows, h * GLA_DV:(h + 1) * GLA_DV] = _dot_nt(qe[:, ks], st.astype(BF16)) + _dot(att, v)
            states[h] = st * elast[:, ks] + _dot_tn(v, kd[:, ks])
    for h in range(GLA_HEADS):
        st_scr[h] = states[h]

    r = proj_scr[:, 2 * DK_ALL + DV_ALL:QKVR]
    mix = _gla_out(oscan_scr[...], r, gout_ref[...], wout_ref)
    o_ref[0] = x + _get_mod(gt_ref, True) * mix

    @pl.when(i == pl.num_programs(1) - 1)
    def _():
        for h in range(GLA_HEADS):
            sout_ref[0, h] = st_scr[h].T


def _gla_prompt(x, mod_all, g, wqkvr, wglr, wg2, bg, gout, wout, s0):
    tok = _Tok(True, TM_GLA)
    state_spec = pl.BlockSpec((1, GLA_HEADS, GLA_DK, GLA_DV), lambda b, i: (b, 0, 0, 0))
    return pl.pallas_call(
        _gla_prompt_kernel,
        grid=tok.grid,
        in_specs=[tok.x(D_MODEL), tok.mod(_mod_col(0, 1, 0)), tok.mod(_mod_col(0, 1, 1)), tok.mod(_mod_col(0, 1, 2)),
                  _resident((1, D_MODEL)), _resident(wqkvr.shape), _resident(wglr.shape), _resident(wg2.shape),
                  _resident(bg.shape), _resident(gout.shape), _resident(wout.shape), state_spec],
        out_specs=[tok.x(D_MODEL), state_spec],
        out_shape=[jax.ShapeDtypeStruct(x.shape, F32), jax.ShapeDtypeStruct(s0.shape, F32)],
        scratch_shapes=[pltpu.VMEM((TM_GLA, QKVR), F32), pltpu.VMEM((TM_GLA, DK_ALL), F32),
                        pltpu.VMEM((TM_GLA, DV_ALL), F32), pltpu.VMEM((GLA_HEADS, GLA_DV, GLA_DK), F32)],
        compiler_params=_params(),
        name="gla_prompt",
    )(x, mod_all, mod_all, mod_all, g, wqkvr, wglr, wg2, bg, gout, wout, s0)


def _gla_in_sample_kernel(x_ref, sh_ref, sc_ref, g_ref, wqkvr_ref, wglr_ref, wg2_ref, bg_ref, proj_ref, la_ref):
    x = x_ref[...]
    g_, r_, _ = x.shape
    u = _modulate(x, g_ref[...], _get_mod(sh_ref, False), _get_mod(sc_ref, False))
    u = u.reshape(g_ * r_, D_MODEL).astype(BF16)
    proj, log_a = _gla_in(u, wqkvr_ref, wglr_ref, wg2_ref, bg_ref)
    proj_ref[...] = proj.reshape(g_, r_, QKVR)
    la_ref[...] = log_a.reshape(g_, r_, DK_ALL)


def _gla_scan_sample_kernel(proj_ref, la_ref, s0_ref, o_ref, sout_ref, xt_scr):
    qscale = GLA_DK ** -0.5
    n_kind = DEC_SEQ * GLA_HEADS
    xt_scr[...] = jnp.zeros_like(xt_scr)

    def seq(j, carry):
        b = None
        for t in range(DEC_SEQ):
            la_t = la_ref[t, pl.ds(j, 1), :]
            b = la_t if b is None else b + la_t
            q_t = proj_ref[t, pl.ds(j, 1), 0:DK_ALL] * qscale * jnp.exp(b)
            k_t = proj_ref[t, pl.ds(j, 1), DK_ALL:2 * DK_ALL] * jnp.exp(-b)
            for h in range(GLA_HEADS):
                ks = slice(h * GLA_DK, (h + 1) * GLA_DK)
                row = h * DEC_SEQ + t
                xt_scr[row:row + 1, :] = q_t[:, ks]
                xt_scr[n_kind + row:n_kind + row + 1, :] = k_t[:, ks]
        total = jnp.exp(b)
        for h in range(GLA_HEADS):
            xt_scr[2 * n_kind + h:2 * n_kind + h + 1, :] = total[:, h * GLA_DK:(h + 1) * GLA_DK]
        xt = xt_scr[...].T
        for h in range(GLA_HEADS):
            u = s0_ref[j, h]
            for t in range(DEC_SEQ):
                row = h * DEC_SEQ + t
                q_c = xt[:, row:row + 1]
                k_c = xt[:, n_kind + row:n_kind + row + 1]
                vlo = 2 * DK_ALL + h * GLA_DV
                v_t = proj_ref[t, pl.ds(j, 1), vlo:vlo + GLA_DV]
                u = u + k_c * v_t
                o_ref[t, pl.ds(j, 1), h * GLA_DV:(h + 1) * GLA_DV] = jnp.sum(q_c * u, axis=0, keepdims=True)
            sout_ref[j, h] = xt[:, 2 * n_kind + h:2 * n_kind + h + 1] * u
        return carry

    lax.fori_loop(0, SCAN_BS, seq, 0)


def _gla_out_sample_kernel(x_ref, gt_ref, oscan_ref, r_ref, gout_ref, wout_ref, o_ref):
    x = x_ref[...]
    g_, r_, _ = x.shape
    n = g_ * r_
    mix = _gla_out(oscan_ref[...].reshape(n, DV_ALL), r_ref[...].reshape(n, DV_ALL), gout_ref[...], wout_ref)
    o_ref[...] = x + _get_mod(gt_ref, False) * mix.reshape(g_, r_, D_MODEL)


def _gla_sample(x, mod_all, g, wqkvr, wglr, wg2, bg, gout, wout, s0):
    tok = _Tok(False, 0)
    proj, log_a = pl.pallas_call(
        _gla_in_sample_kernel,
        grid=tok.grid,
        in_specs=[tok.x(D_MODEL), tok.mod(_mod_col(0, 1, 0)), tok.mod(_mod_col(0, 1, 1)),
                  _resident((1, D_MODEL)), _resident(wqkvr.shape), _resident(wglr.shape), _resident(wg2.shape),
                  _resident(bg.shape)],
        out_specs=[tok.x(QKVR), tok.x(DK_ALL)],
        out_shape=[jax.ShapeDtypeStruct(tok.shape(QKVR), F32), jax.ShapeDtypeStruct(tok.shape(DK_ALL), F32)],
        compiler_params=_params(),
        name="gla_in_sample",
    )(x, mod_all, mod_all, g, wqkvr, wglr, wg2, bg)

    state_spec = pl.BlockSpec((SCAN_BS, GLA_HEADS, GLA_DK, GLA_DV), lambda j: (j, 0, 0, 0))
    oscan, s_out = pl.pallas_call(
        _gla_scan_sample_kernel,
        grid=(DEC_BATCH // SCAN_BS,),
        in_specs=[pl.BlockSpec((DEC_SEQ, SCAN_BS, QKVR), lambda j: (0, j, 0)),
                  pl.BlockSpec((DEC_SEQ, SCAN_BS, DK_ALL), lambda j: (0, j, 0)),
                  state_spec],
        out_specs=[pl.BlockSpec((DEC_SEQ, SCAN_BS, DV_ALL), lambda j: (0, j, 0)), state_spec],
        out_shape=[jax.ShapeDtypeStruct(tok.shape(DV_ALL), F32), jax.ShapeDtypeStruct(s0.shape, F32)],
        scratch_shapes=[pltpu.VMEM((LANE, LANE), F32)],
        compiler_params=pltpu.CompilerParams(dimension_semantics=("arbitrary",), vmem_limit_bytes=VMEM_LIMIT),
        name="gla_scan_sample",
    )(proj, log_a, s0)

    h = pl.pallas_call(
        _gla_out_sample_kernel,
        grid=tok.grid,
        in_specs=[tok.x(D_MODEL), tok.mod(_mod_col(0, 1, 2)), tok.x(DV_ALL),
                  tok.x(DV_ALL, col=(2 * DK_ALL + DV_ALL) // DV_ALL), _resident(gout.shape), _resident(wout.shape)],
        out_specs=tok.x(D_MODEL),
        out_shape=jax.ShapeDtypeStruct(x.shape, F32),
        compiler_params=_params(),
        name="gla_out_sample",
    )(x, mod_all, oscan, proj, gout, wout)
    return h, s_out


def _kv_kernel(x_ref, sh_ref, sc_ref, g_ref, wkv_ref, wf_ref, bf_ref, gk_ref, *refs, prompt):
    if prompt:
        k_ref, v_ref, lf_ref, kb_ref, vb_ref, f_ref, ft_ref, carry_scr = refs
    else:
        k_ref, v_ref, lf_ref = refs
    x = x_ref[...]
    g_, r_, _ = x.shape
    n = g_ * r_
    u = _modulate(x, g_ref[...], _get_mod(sh_ref, prompt), _get_mod(sc_ref, prompt))
    u = u.reshape(n, D_MODEL).astype(BF16)
    kv = _dot(u, wkv_ref[...])
    k = kv[:, :D_MODEL]
    v = kv[:, D_MODEL:]
    kn = k * _head_inv_rms(k) * gk_ref[...]
    lf = _log_sigmoid(_dot(u, wf_ref[...]) + bf_ref[...])
    k_ref[...] = kn.reshape(g_, r_, D_MODEL)
    v_ref[...] = v.reshape(g_, r_, D_MODEL)
    lf_ref[...] = lf[:, :FOX_HEADS].reshape(g_, r_, FOX_HEADS)
    if prompt:
        kb_ref[...] = kn.astype(BF16).reshape(g_, r_, D_MODEL)
        vb_ref[...] = v.astype(BF16).reshape(g_, r_, D_MODEL)

        @pl.when(pl.program_id(1) == 0)
        def _():
            carry_scr[...] = jnp.zeros_like(carry_scr)

        tri = jnp.where(_lower_tri(n), 1.0, 0.0).astype(BF16)
        fsum = _split_dot(tri, lf, 3) + carry_scr[...]
        carry_scr[...] = fsum[n - 1:n, :]
        f_ref[...] = fsum[:, :FOX_HEADS].reshape(g_, r_, FOX_HEADS)
        ft_ref[0] = fsum.T[:FOX_HEADS, :]


def _kv(tok, x, mod_kv, g, wkv, wf, bf, gk, name):
    prompt = tok.prompt
    out_specs = [tok.x(D_MODEL), tok.x(D_MODEL), tok.x(FOX_HEADS)]
    out_shape = [jax.ShapeDtypeStruct(tok.shape(D_MODEL), F32), jax.ShapeDtypeStruct(tok.shape(D_MODEL), F32),
                 jax.ShapeDtypeStruct(tok.shape(FOX_HEADS), F32)]
    scratch = []
    if prompt:
        out_specs += [tok.x(D_MODEL), tok.x(D_MODEL), tok.x(FOX_HEADS),
                      pl.BlockSpec((1, FOX_HEADS, TM_KV), lambda b, i: (b, 0, i))]
        out_shape += [jax.ShapeDtypeStruct(tok.shape(D_MODEL), BF16), jax.ShapeDtypeStruct(tok.shape(D_MODEL), BF16),
                      jax.ShapeDtypeStruct(tok.shape(FOX_HEADS), F32),
                      jax.ShapeDtypeStruct((BATCH, FOX_HEADS, SEQ), F32)]
        scratch = [pltpu.VMEM((1, LANE), F32)]
    return pl.pallas_call(
        functools.partial(_kv_kernel, prompt=prompt),
        grid=tok.grid,
        in_specs=[tok.x(D_MODEL), tok.mod(0), tok.mod(1), _resident((1, D_MODEL)), _resident(wkv.shape),
                  _resident(wf.shape), _resident(bf.shape), _resident(gk.shape)],
        out_specs=out_specs,
        out_shape=out_shape,
        scratch_shapes=scratch,
        compiler_params=_params(),
        name=name,
    )(x, mod_kv, mod_kv, g, wkv, wf, bf, gk)


def _fox_q(u, wqg_ref, gq_ref):
    qg = _dot(u, wqg_ref[...])
    q = qg[:, :D_MODEL]
    og = qg[:, D_MODEL:]
    qn = q * _head_inv_rms(q) * gq_ref[...] * (FOX_HD ** -0.5 * LOG2E)
    return qn, og


def _softmax_step(t, row_bias, m, l, acc, v):
    m_cur = jnp.max(t, axis=-1, keepdims=True)
    if row_bias is not None:
        m_cur = m_cur + row_bias
    m_new = jnp.maximum(m, m_cur)
    alpha = jnp.exp2(m - m_new)
    shift = -m_new if row_bias is None else row_bias - m_new
    p = jnp.exp2(t + jnp.tile(shift, (1, t.shape[1] // LANE)))
    l_new = alpha * l + jnp.sum(p, axis=-1, keepdims=True)
    pv = _dot(p.astype(BF16), v)
    acc_new = jnp.tile(alpha, (1, pv.shape[1] // LANE)) * acc + pv
    return m_new, l_new, acc_new


def _fox_prompt_kernel(x_ref, sh_ref, sc_ref, gt_ref, g_ref, wqg_ref, gq_ref, wo_ref, kb_ref, vb_ref, f_ref, ft_ref,
                       o_ref, qm_scr, og_scr, oatt_scr, m_scr, l_scr, acc_scr, fq_scr):
    i = pl.program_id(1)
    tq = x_ref.shape[1]
    x = x_ref[0]
    u = _modulate(x, g_ref[...], _get_mod(sh_ref, True), _get_mod(sc_ref, True)).astype(BF16)
    qn, og = _fox_q(u, wqg_ref, gq_ref)
    og_scr[...] = og
    first = lax.broadcasted_iota(jnp.int32, (tq, LANE), 1) < FOX_HD

    for pair in range(FOX_HEADS // 2):
        q2 = qn[:, pair * LANE:(pair + 1) * LANE]
        qm_scr[2 * pair] = jnp.where(first, q2, 0.0).astype(BF16)
        qm_scr[2 * pair + 1] = jnp.where(first, 0.0, q2).astype(BF16)

    fq_all = f_ref[0]
    causal = _lower_tri(tq)

    for grp in range(FOX_HEADS // FOX_GROUP):
        m_scr[...] = jnp.full_like(m_scr, NEG)
        l_scr[...] = jnp.zeros_like(l_scr)
        acc_scr[...] = jnp.zeros_like(acc_scr)
        for idx in range(FOX_GROUP):
            head = grp * FOX_GROUP + idx
            fq_scr[idx] = jnp.broadcast_to(fq_all[:, head:head + 1], (tq, LANE)) * LOG2E

        def step(j, diagonal, grp=grp):
            rows = pl.ds(pl.multiple_of(j * tq, tq), tq)
            for idx in range(FOX_GROUP):
                head = grp * FOX_GROUP + idx
                cols = slice((head // 2) * LANE, (head // 2 + 1) * LANE)
                fk = ft_ref[0, head, pl.ds(j, 1), :] * LOG2E
                t = _dot_nt(qm_scr[head], kb_ref[0, rows, cols]) - fk
                if diagonal:
                    t = jnp.where(causal, t, NEG)
                m, l, acc = _softmax_step(t, fq_scr[idx], m_scr[idx], l_scr[idx], acc_scr[idx],
                                          vb_ref[0, rows, cols])
                m_scr[idx] = m
                l_scr[idx] = l
                acc_scr[idx] = acc

        def off_diagonal(j, carry):
            step(j, False)
            return carry

        lax.fori_loop(0, i, off_diagonal, 0)
        step(i, True)
        for pp in range(FOX_GROUP // 2):
            pair = grp * (FOX_GROUP // 2) + pp
            o2 = jnp.where(first, acc_scr[2 * pp] / l_scr[2 * pp], acc_scr[2 * pp + 1] / l_scr[2 * pp + 1])
            oatt_scr[:, pair * LANE:(pair + 1) * LANE] = o2

    gated = (oatt_scr[...] * jax.nn.sigmoid(og_scr[...])).astype(BF16)
    o_ref[0] = x + _get_mod(gt_ref, True) * _dot(gated, wo_ref[...])


def _fox_prompt(x, mod_all, g, wqg, gq, wo, kb, vb, fsum, fsum_t):
    tok = _Tok(True, TQ_FOX)
    nk = SEQ // TQ_FOX
    seq_spec = pl.BlockSpec((1, SEQ, D_MODEL), lambda b, i: (b, 0, 0), pipeline_mode=pl.Buffered(1))
    return pl.pallas_call(
        _fox_prompt_kernel,
        grid=tok.grid,
        in_specs=[tok.x(D_MODEL), tok.mod(_mod_col(1, 1, 0)), tok.mod(_mod_col(1, 1, 1)), tok.mod(_mod_col(1, 1, 2)),
                  _resident((1, D_MODEL)), _resident(wqg.shape), _resident(gq.shape), _resident(wo.shape),
                  seq_spec, seq_spec, tok.x(FOX_HEADS),
                  pl.BlockSpec((1, FOX_HEADS, nk, TQ_FOX), lambda b, i: (b, 0, 0, 0))],
        out_specs=tok.x(D_MODEL),
        out_shape=jax.ShapeDtypeStruct(x.shape, F32),
        scratch_shapes=[pltpu.VMEM((FOX_HEADS, TQ_FOX, LANE), BF16), pltpu.VMEM((TQ_FOX, D_MODEL), F32),
                        pltpu.VMEM((TQ_FOX, D_MODEL), F32), pltpu.VMEM((FOX_GROUP, TQ_FOX, LANE), F32),
                        pltpu.VMEM((FOX_GROUP, TQ_FOX, LANE), F32), pltpu.VMEM((FOX_GROUP, TQ_FOX, LANE), F32),
                        pltpu.VMEM((FOX_GROUP, TQ_FOX, LANE), F32)],
        compiler_params=_params(),
        name="fox_prompt",
    )(x, mod_all, mod_all, mod_all, g, wqg, gq, wo, kb, vb, fsum, fsum_t.reshape(BATCH, FOX_HEADS, nk, TQ_FOX))


def _fox_q_sample_kernel(x_ref, sh_ref, sc_ref, g_ref, wqg_ref, gq_ref, q_ref, og_ref):
    x = x_ref[...]
    g_, r_, _ = x.shape
    u = _modulate(x, g_ref[...], _get_mod(sh_ref, False), _get_mod(sc_ref, False))
    qn, og = _fox_q(u.reshape(g_ * r_, D_MODEL).astype(BF16), wqg_ref, gq_ref)
    q_ref[...] = qn.reshape(g_, r_, D_MODEL)
    og_ref[...] = og.reshape(g_, r_, D_MODEL)


def _fox_out_sample_kernel(x_ref, gt_ref, oatt_ref, og_ref, wo_ref, o_ref):
    x = x_ref[...]
    g_, r_, _ = x.shape
    n = g_ * r_
    gated = (oatt_ref[...] * jax.nn.sigmoid(og_ref[...])).reshape(n, D_MODEL).astype(BF16)
    o_ref[...] = x + _get_mod(gt_ref, False) * _dot(gated, wo_ref[...]).reshape(g_, r_, D_MODEL)


def _paged_kernel(pt_ref, q_ref, kn_ref, vn_ref, lnt_ref, *refs):
    del pt_ref
    k_pages = refs[:N_PAGES]
    v_pages = refs[N_PAGES:2 * N_PAGES]
    l_pages = refs[2 * N_PAGES:3 * N_PAGES]
    o_ref = refs[3 * N_PAGES]
    kn_scr, vn_scr = refs[3 * N_PAGES + 1:]
    nrow = DEC_SEQ * FOX_HEADS

    hrow = lax.broadcasted_iota(jnp.int32, (FOX_HEADS, D_MODEL), 0)
    hcol = lax.broadcasted_iota(jnp.int32, (FOX_HEADS, D_MODEL), 1)
    head_mask = (hcol >> HD_SHIFT) == hrow
    q = q_ref[0]
    qbd = jnp.concatenate(
        [jnp.where(head_mask, jnp.broadcast_to(q[t:t + 1, :], (FOX_HEADS, D_MODEL)), 0.0) for t in range(DEC_SEQ)],
        axis=0).astype(BF16)

    lnt = lnt_ref[0]
    cn = [lnt[:, 0:1]]
    for t in range(1, DEC_SEQ):
        cn.append(cn[-1] + lnt[:, t:t + 1])
    cn_col = jnp.concatenate(cn, axis=0)

    ri = lax.broadcasted_iota(jnp.int32, (PAGE_SIZE, 2 * LANE), 0)
    ci = lax.broadcasted_iota(jnp.int32, (PAGE_SIZE, 2 * LANE), 1)
    suffix = jnp.where(((ci < PAGE_SIZE) & (ri > ci)) | (ci == PAGE_SIZE), 1.0, 0.0).astype(BF16)

    logits = [None] * N_PAGES
    tot = jnp.zeros((FOX_HEADS, 1), F32)
    for idx in reversed(range(N_PAGES)):
        sums = _split_dot_lhs(l_pages[idx][0], suffix, 2)
        bias16 = sums[:, :PAGE_SIZE] + tot
        tot = tot + sums[:, PAGE_SIZE:PAGE_SIZE + 1]
        bias = jnp.concatenate([bias16] * DEC_SEQ, axis=0) + cn_col
        logits[idx] = _dot(qbd, k_pages[idx][0].astype(BF16)) + bias * LOG2E

    kn_scr[...] = jnp.zeros_like(kn_scr)
    vn_scr[...] = jnp.zeros_like(vn_scr)
    kn_scr[0:DEC_SEQ, :] = kn_ref[0]
    vn_scr[0:DEC_SEQ, :] = vn_ref[0]
    lane = lax.broadcasted_iota(jnp.int32, (nrow, LANE), 1)
    step_of_row = lax.broadcasted_iota(jnp.int32, (nrow, LANE), 0) >> HEAD_SHIFT
    cn_keys = jnp.zeros((nrow, LANE), F32)
    for t in range(DEC_SEQ):
        cn_keys = jnp.where(lane == t, jnp.concatenate([cn[t]] * DEC_SEQ, axis=0), cn_keys)
    s_new = _dot_nt(qbd, kn_scr[...].astype(BF16)) + (cn_col - cn_keys) * LOG2E
    logits.append(jnp.where(lane <= step_of_row, s_new, NEG))

    m = logits[0]
    for s in logits[1:]:
        m = jnp.maximum(m, s)
    m = jnp.broadcast_to(jnp.max(m, axis=-1, keepdims=True), (nrow, LANE))
    l = jnp.zeros((nrow, LANE), F32)
    acc = jnp.zeros((nrow, D_MODEL), F32)
    for idx, s in enumerate(logits):
        p = jnp.exp2(s - m)
        l = l + p
        if idx < N_PAGES:
            acc = acc + _dot_nt(p.astype(BF16), v_pages[idx][0].astype(BF16))
        else:
            acc = acc + _dot(p.astype(BF16), vn_scr[...].astype(BF16))
    l = jnp.broadcast_to(jnp.sum(l, axis=-1, keepdims=True), (nrow, LANE))
    out = acc / jnp.tile(l, (1, D_MODEL // LANE))
    for t in range(DEC_SEQ):
        blk = out[t * FOX_HEADS:(t + 1) * FOX_HEADS, :]
        o_ref[0, t:t + 1, :] = jnp.sum(jnp.where(head_mask, blk, 0.0), axis=0, keepdims=True)


def _paged_attend(q, k_new, v_new, lf_new_t, cache_k_t, cache_v_t, cache_lf_t, page_table):
    def page_spec(shape, idx):
        return pl.BlockSpec((1,) + shape, lambda b, pt: (pt[b, idx], 0, 0))

    seq_spec = pl.BlockSpec((1, DEC_SEQ, D_MODEL), lambda b, pt: (b, 0, 0))
    in_specs = [seq_spec, seq_spec, seq_spec, pl.BlockSpec((1, FOX_HEADS, DEC_SEQ), lambda b, pt: (b, 0, 0))]
    in_specs += [page_spec((D_MODEL, PAGE_SIZE), i) for i in range(N_PAGES)]
    in_specs += [page_spec((D_MODEL, PAGE_SIZE), i) for i in range(N_PAGES)]
    in_specs += [page_spec((FOX_HEADS, PAGE_SIZE), i) for i in range(N_PAGES)]
    grid_spec = pltpu.PrefetchScalarGridSpec(
        num_scalar_prefetch=1,
        grid=(DEC_BATCH,),
        in_specs=in_specs,
        out_specs=seq_spec,
        scratch_shapes=[pltpu.VMEM((PAGE_SIZE, D_MODEL), F32), pltpu.VMEM((PAGE_SIZE, D_MODEL), F32)],
    )
    return pl.pallas_call(
        _paged_kernel,
        grid_spec=grid_spec,
        out_shape=jax.ShapeDtypeStruct((DEC_BATCH, DEC_SEQ, D_MODEL), F32),
        compiler_params=pltpu.CompilerParams(dimension_semantics=("arbitrary",), vmem_limit_bytes=VMEM_LIMIT),
        name="fox_paged",
    )(page_table, q, k_new, v_new, lf_new_t, *([cache_k_t] * N_PAGES), *([cache_v_t] * N_PAGES),
      *([cache_lf_t] * N_PAGES))


def _fox_sample(x, mod_all, g, wqg, gq, wo, k_new, v_new, lf_new, cache_k, cache_v, cache_logf, page_table):
    tok = _Tok(False, 0)
    qn, og = pl.pallas_call(
        _fox_q_sample_kernel,
        grid=tok.grid,
        in_specs=[tok.x(D_MODEL), tok.mod(_mod_col(1, 1, 0)), tok.mod(_mod_col(1, 1, 1)), _resident((1, D_MODEL)),
                  _resident(wqg.shape), _resident(gq.shape)],
        out_specs=[tok.x(D_MODEL), tok.x(D_MODEL)],
        out_shape=[jax.ShapeDtypeStruct(x.shape, F32), jax.ShapeDtypeStruct(x.shape, F32)],
        compiler_params=_params(),
        name="fox_q_sample",
    )(x, mod_all, mod_all, g, wqg, gq)

    n_phys = cache_k.shape[0]
    oatt = _paged_attend(
        qn.transpose(1, 0, 2), k_new.transpose(1, 0, 2), v_new.transpose(1, 0, 2), lf_new.transpose(1, 2, 0),
        cache_k.transpose(0, 2, 3, 1).reshape(n_phys, D_MODEL, PAGE_SIZE),
        cache_v.transpose(0, 2, 3, 1).reshape(n_phys, D_MODEL, PAGE_SIZE),
        cache_logf.transpose(0, 2, 1), page_table)

    return pl.pallas_call(
        _fox_out_sample_kernel,
        grid=tok.grid,
        in_specs=[tok.x(D_MODEL), tok.mod(_mod_col(1, 1, 2)), tok.x(D_MODEL), tok.x(D_MODEL), _resident(wo.shape)],
        out_specs=tok.x(D_MODEL),
        out_shape=jax.ShapeDtypeStruct(x.shape, F32),
        compiler_params=_params(),
        name="fox_out_sample",
    )(x, mod_all, oatt.transpose(1, 0, 2), og, wo)


def kernel(x_prompt, x_sample, state_gla, cache_k, cache_v, cache_logf, page_table, c_prompt, c_sample, w_ada, b_ada, g_norm, w_ffn_up, w_ffn_down, gla_w_in, gla_w_gate2, gla_b_gate, gla_g_out, gla_w_out, w_ada_kv, b_ada_kv, g_kv, w_kvf, b_f, g_k, fox_w_qg, fox_g_q, fox_w_o):
    c_all = jnp.concatenate([c_sample, c_prompt, jnp.zeros((MOD_ROWS - DEC_BATCH - BATCH, D_MODEL), F32)], axis=0)
    wu = w_ffn_up.astype(BF16)
    wd = w_ffn_down.astype(BF16)
    w_in = gla_w_in[0]
    wqkvr = w_in[:, :QKVR].astype(BF16)
    wglr = jnp.pad(w_in[:, QKVR:], ((0, 0), (0, LANE - GLA_RANK))).astype(BF16)
    wg2 = jnp.pad(gla_w_gate2[0], ((0, LANE - GLA_RANK), (0, 0))).astype(BF16)
    bg = gla_b_gate[0][None, :]
    gout = gla_g_out[0][None, :]
    wout = gla_w_out[0].astype(BF16)
    wkv = w_kvf[:, :2 * D_MODEL].astype(BF16)
    wf = jnp.pad(w_kvf[:, 2 * D_MODEL:], ((0, 0), (0, LANE - FOX_HEADS))).astype(BF16)
    bf = jnp.pad(b_f, (0, LANE - FOX_HEADS))[None, :]
    gk = jnp.tile(g_k, FOX_HEADS)[None, :]
    wqg = fox_w_qg[0].astype(BF16)
    gq = jnp.tile(fox_g_q[0], FOX_HEADS)[None, :]
    wo = fox_w_o[0].astype(BF16)
    gkv = g_kv[None, :]

    def gn(layer, sub):
        return g_norm[layer, sub][None, :]

    mod_all = _ada(c_all, w_ada, b_ada[:, None, :], "ada_mod")
    mod_kv = _ada(c_all, w_ada_kv[None], b_ada_kv[None, None, :], "ada_mod_kv")

    tok = _Tok(True, TM_FFN)
    h = _ffn(tok, x_prompt, mod_all, 0, 0, gn(0, 0), wu, wd,"ffn_p00")
    s0 = jnp.zeros((BATCH, GLA_HEADS, GLA_DK, GLA_DV), F32)
    h, sg_prompt = _gla_prompt(h, mod_all, gn(0, 1), wqkvr, wglr, wg2, bg, gout, wout, s0)
    h = _ffn(tok, h, mod_all, 0, 2, gn(0, 2), wu, wd,"ffn_p02")
    k_p, v_p, lf_p, kb, vb, fsum, fsum_t = _kv(_Tok(True, TM_KV), h, mod_kv, gkv, wkv, wf, bf, gk, "kv_prompt")
    h = _ffn(tok, h, mod_all, 1, 0, gn(1, 0), wu, wd,"ffn_p10")
    h = _fox_prompt(h, mod_all, gn(1, 1), wqg, gq, wo, kb, vb, fsum, fsum_t)
    y_prompt = _ffn(tok, h, mod_all, 1, 2, gn(1, 2), wu, wd,"ffn_p12")

    tok = _Tok(False, 0)
    hs = x_sample.transpose(1, 0, 2)
    hs = _ffn(tok, hs, mod_all, 0, 0, gn(0, 0), wu, wd,"ffn_s00")
    hs, sg_sample = _gla_sample(hs, mod_all, gn(0, 1), wqkvr, wglr, wg2, bg, gout, wout, state_gla[0])
    hs = _ffn(tok, hs, mod_all, 0, 2, gn(0, 2), wu, wd,"ffn_s02")
    k_s, v_s, lf_s = _kv(tok, hs, mod_kv, gkv, wkv, wf, bf, gk, "kv_sample")
    hs = _ffn(tok, hs, mod_all, 1, 0, gn(1, 0), wu, wd,"ffn_s10")
    hs = _fox_sample(hs, mod_all, gn(1, 1), wqg, gq, wo, k_s, v_s, lf_s, cache_k, cache_v, cache_logf, page_table)
    hs = _ffn(tok, hs, mod_all, 1, 2, gn(1, 2), wu, wd,"ffn_s12")
    y_sample = hs.transpose(1, 0, 2)

    def heads(t, lead):
        return t.reshape(lead + (FOX_HEADS, FOX_HD))

    return (y_prompt, y_sample, sg_prompt[None],
            heads(k_p, (BATCH, SEQ)), heads(v_p, (BATCH, SEQ)), lf_p,
            sg_sample[None],
            heads(k_s.transpose(1, 0, 2), (DEC_BATCH, DEC_SEQ)), heads(v_s.transpose(1, 0, 2), (DEC_BATCH, DEC_SEQ)),
            lf_s.transpose(1, 0, 2))
```
